```python
import math
import jax, jax.numpy as jnp
from jax import lax
import numpy as np

D_MODEL = 2048
BATCH = 4
SEQ = 2048
DEPTH = 2

EXPAND = 2
MIX_WIDTH = EXPAND * D_MODEL
SSD_WIDTH = MIX_WIDTH // 2
SSD_HEAD_DIM = 64
SSD_HEADS = SSD_WIDTH // SSD_HEAD_DIM
SSD_GROUPS = 8
SSD_STATE = 128
SSD_CONV = 4
SSD_CHUNK = 128
SSD_CONV_DIM = SSD_WIDTH + 2 * SSD_GROUPS * SSD_STATE
SGU_WIDTH = MIX_WIDTH - SSD_WIDTH
SGU_CHUNK = 128
SGU_GROUPS = 16
SGU_GROUP_DIM = SGU_WIDTH // SGU_GROUPS
EVEN_SPLITS = (SSD_WIDTH, SSD_CONV_DIM, SSD_HEADS, SGU_WIDTH, SGU_WIDTH, SGU_WIDTH)
EVEN_IN = sum(EVEN_SPLITS)
DIFF_HEADS = 16
DIFF_HEAD_DIM = 128
DIFF_V_DIM = 2 * DIFF_HEAD_DIM
DIFF_WIDTH = DIFF_HEADS * DIFF_V_DIM
ODD_IN = 4 * DIFF_WIDTH
Q_BLOCK = 128
EPS = 1e-6
N_EVEN = (DEPTH + 1) // 2
N_ODD = DEPTH // 2

kernel_name = "hybrid_ssd_sgu_diffattn_block"


def rmsnorm(x, w):
    xf = x.astype(jnp.float32)
    y = xf * lax.rsqrt(jnp.mean(xf * xf, axis=-1, keepdims=True) + EPS)
    return (y * w.astype(jnp.float32)).astype(x.dtype)


def layernorm(x, w, b):
    xf = x.astype(jnp.float32)
    mu = jnp.mean(xf, axis=-1, keepdims=True)
    xc = xf - mu
    y = xc * lax.rsqrt(jnp.mean(xc * xc, axis=-1, keepdims=True) + EPS)
    return (y * w.astype(jnp.float32) + b.astype(jnp.float32)).astype(x.dtype)


def causal_depthwise_conv(x, w, b):
    K, C = w.shape
    y = lax.conv_general_dilated(x, w[:, None, :].astype(x.dtype), window_strides=(1,),
                                 padding=[(K - 1, 0)], dimension_numbers=('NWC', 'WIO', 'NWC'),
                                 feature_group_count=C)
    return y + b.astype(x.dtype)


def ssd_chunked(x, dt, a, bmat, cmat, d_skip):
    Bsz, S, H, P = x.shape
    G, N = bmat.shape[2], bmat.shape[3]
    R = H // G
    L = SSD_CHUNK
    nc = S // L
    xd = (x * dt[..., None].astype(x.dtype)).reshape(Bsz, nc, L, G, R, P)
    da = (dt.astype(jnp.float32) * a).reshape(Bsz, nc, L, G, R)
    da = jnp.moveaxis(da, 2, -1)
    cs = jnp.cumsum(da, axis=-1)
    bc = bmat.reshape(Bsz, nc, L, G, N)
    cc = cmat.reshape(Bsz, nc, L, G, N)
    causal = jnp.tril(jnp.ones((L, L), dtype=bool))
    seg = cs[..., :, None] - cs[..., None, :]
    decay = jnp.where(causal, jnp.exp(jnp.where(causal, seg, 0.0)), 0.0)
    cb = jnp.einsum('bclgn,bcsgn->bcgls', cc, bc)
    y_diag = jnp.einsum('bcgls,bcgrls,bcsgrp->bclgrp', cb, decay, xd)
    decay_to_end = jnp.exp(cs[..., -1:] - cs)
    states = jnp.einsum('bclgn,bcgrl,bclgrp->bcgrpn', bc, decay_to_end, xd).astype(jnp.float32)
    chunk_decay = jnp.exp(cs[..., -1])

    def step(h, inp):
        st, dec = inp
        return h * dec[..., None, None] + st, h

    h0 = jnp.zeros((Bsz, G, R, P, N), jnp.float32)
    _, prev = lax.scan(step, h0, (jnp.moveaxis(states, 1, 0), jnp.moveaxis(chunk_decay, 1, 0)))
    prev = jnp.moveaxis(prev, 0, 1)
    y_off = jnp.einsum('bclgn,bcgrpn,bcgrl->bclgrp', cc, prev, jnp.exp(cs))
    y = (y_diag + y_off).reshape(Bsz, S, H, P).astype(x.dtype)
    return y + x * d_skip[:, None].astype(x.dtype)


def even_mixer(h, w_in, conv_w, conv_b, dt_bias, a_log, d_skip, ssd_norm_w,
               sgu_ln_w, sgu_ln_b, sgu_ws, sgu_b, w_out):
    Bsz, S, _ = h.shape
    proj = h @ w_in
    z_a, xbc, dt_raw, z_b, u, v = jnp.split(proj, [int(c) for c in np.cumsum(EVEN_SPLITS)[:-1]], axis=-1)
    xbc = jax.nn.silu(causal_depthwise_conv(xbc, conv_w, conv_b))
    xs, bm, cm = jnp.split(xbc, [SSD_WIDTH, SSD_WIDTH + SSD_GROUPS * SSD_STATE], axis=-1)
    dt = jax.nn.softplus(dt_raw.astype(jnp.float32) + dt_bias.astype(jnp.float32))
    a = -jnp.exp(a_log.astype(jnp.float32))
    y = ssd_chunked(xs.reshape(Bsz, S, SSD_HEADS, SSD_HEAD_DIM), dt, a,
                    bm.reshape(Bsz, S, SSD_GROUPS, SSD_STATE),
                    cm.reshape(Bsz, S, SSD_GROUPS, SSD_STATE), d_skip)
    y = y.reshape(Bsz, S, SSD_WIDTH) * jax.nn.silu(z_a)
    y_a = rmsnorm(y.reshape(Bsz, S, SSD_GROUPS, SSD_WIDTH // SSD_GROUPS),
                  ssd_norm_w.reshape(SSD_GROUPS, -1)).reshape(Bsz, S, SSD_WIDTH)
    u = jax.nn.gelu(u)
    v = layernorm(jax.nn.gelu(v), sgu_ln_w, sgu_ln_b)
    nc = S // SGU_CHUNK
    vg = v.reshape(Bsz, nc, SGU_CHUNK, SGU_GROUPS, SGU_GROUP_DIM)
    ws = sgu_ws * jnp.tril(jnp.ones((SGU_CHUNK, SGU_CHUNK), sgu_ws.dtype))
    mixed = jnp.einsum('gts,bnsgc->bntgc', ws, vg) + sgu_b.T[:, :, None]
    y_b = u * mixed.reshape(Bsz, S, SGU_WIDTH) * jax.nn.silu(z_b)
    return jnp.concatenate([y_a, y_b], axis=-1) @ w_out


def odd_mixer(h, w_in, lam_q1, lam_k1, lam_q2, lam_k2, subln_w, w_out, lambda_init):
    Bsz, S, _ = h.shape
    proj = h @ w_in
    q, k, v, g = jnp.split(proj, 4, axis=-1)
    q = q.reshape(Bsz, S, DIFF_HEADS, 2, DIFF_HEAD_DIM).transpose(0, 2, 3, 1, 4)
    k = k.reshape(Bsz, S, DIFF_HEADS, 2, DIFF_HEAD_DIM).transpose(0, 2, 3, 1, 4)
    v = v.reshape(Bsz, S, DIFF_HEADS, DIFF_V_DIM).transpose(0, 2, 1, 3)
    lam = (jnp.exp(jnp.sum(lam_q1.astype(jnp.float32) * lam_k1.astype(jnp.float32)))
           - jnp.exp(jnp.sum(lam_q2.astype(jnp.float32) * lam_k2.astype(jnp.float32)))
           + lambda_init)
    scale = DIFF_HEAD_DIM ** -0.5
    outs = []
    for i in range(S // Q_BLOCK):
        end = (i + 1) * Q_BLOCK
        qb = q[:, :, :, i * Q_BLOCK:end]
        kb = k[:, :, :, :end]
        s = jnp.einsum('bhjqd,bhjkd->bhjqk', qb, kb).astype(jnp.float32) * scale
        qpos = i * Q_BLOCK + jnp.arange(Q_BLOCK)
        kpos = jnp.arange(end)
        s = jnp.where(kpos[None, :] <= qpos[:, None], s, -jnp.inf)
        p = jax.nn.softmax(s, axis=-1)
        attn = p[:, :, 0] - lam * p[:, :, 1]
        outs.append(jnp.einsum('bhqk,bhkd->bhqd', attn.astype(v.dtype), v[:, :, :end]))
    o = jnp.concatenate(outs, axis=2)
    o = rmsnorm(o, subln_w) * (1.0 - lambda_init)
    o = o.transpose(0, 2, 1, 3).reshape(Bsz, S, DIFF_WIDTH) * jax.nn.silu(g)
    return o @ w_out


def setup_inputs(seed: int = 0) -> dict:
    key = jax.random.key(seed)
    ks = jax.random.split(key, 24)
    f32 = jnp.float32
    nrm = lambda k, shape, s: jax.random.normal(k, shape, f32) * s
    dt0 = jnp.exp(jax.random.uniform(ks[5], (N_EVEN, SSD_HEADS), f32, math.log(1e-3), math.log(1e-1)))
    return {
        "x": nrm(ks[0], (BATCH, SEQ, D_MODEL), 1.0),
        "norm_w": 1.0 + nrm(ks[1], (DEPTH, D_MODEL), 0.02),
        "even_w_in": nrm(ks[2], (N_EVEN, D_MODEL, EVEN_IN), D_MODEL ** -0.5),
        "even_conv_w": nrm(ks[3], (N_EVEN, SSD_CONV, SSD_CONV_DIM), SSD_CONV ** -0.5),
        "even_conv_b": nrm(ks[4], (N_EVEN, SSD_CONV_DIM), 0.02),
        "even_dt_bias": dt0 + jnp.log(-jnp.expm1(-dt0)),
        "even_a_log": jnp.log(jax.random.uniform(ks[6], (N_EVEN, SSD_HEADS), f32, 1.0, 16.0)),
        "even_d_skip": 1.0 + nrm(ks[7], (N_EVEN, SSD_HEADS), 0.02),
        "even_ssd_norm_w": 1.0 + nrm(ks[8], (N_EVEN, SSD_WIDTH), 0.02),
        "even_sgu_ln_w": 1.0 + nrm(ks[9], (N_EVEN, SGU_WIDTH), 0.02),
        "even_sgu_ln_b": nrm(ks[10], (N_EVEN, SGU_WIDTH), 0.02),
        "even_sgu_ws": nrm(ks[11], (N_EVEN, SGU_GROUPS, SGU_CHUNK, SGU_CHUNK), SGU_CHUNK ** -0.5),
        "even_sgu_b": 1.0 + nrm(ks[12], (N_EVEN, SGU_GROUPS, SGU_CHUNK), 0.02),
        "even_w_out": nrm(ks[13], (N_EVEN, MIX_WIDTH, D_MODEL), MIX_WIDTH ** -0.5),
        "odd_w_in": nrm(ks[14], (N_ODD, D_MODEL, ODD_IN), D_MODEL ** -0.5),
        "odd_lam_q1": nrm(ks[15], (N_ODD, DIFF_HEAD_DIM), 0.1),
        "odd_lam_k1": nrm(ks[16], (N_ODD, DIFF_HEAD_DIM), 0.1),
        "odd_lam_q2": nrm(ks[17], (N_ODD, DIFF_HEAD_DIM), 0.1),
        "odd_lam_k2": nrm(ks[18], (N_ODD, DIFF_HEAD_DIM), 0.1),
        "odd_subln_w": 1.0 + nrm(ks[19], (N_ODD, DIFF_V_DIM), 0.02),
        "odd_w_out": nrm(ks[20], (N_ODD, DIFF_WIDTH, D_MODEL), DIFF_WIDTH ** -0.5),
        "final_norm_w": 1.0 + nrm(ks[21], (D_MODEL,), 0.02),
    }


def reference(x, norm_w, even_w_in, even_conv_w, even_conv_b, even_dt_bias, even_a_log,
              even_d_skip, even_ssd_norm_w, even_sgu_ln_w, even_sgu_ln_b, even_sgu_ws,
              even_sgu_b, even_w_out, odd_w_in, odd_lam_q1, odd_lam_k1, odd_lam_q2,
              odd_lam_k2, odd_subln_w, odd_w_out, final_norm_w):
    h = x
    for layer in range(DEPTH):
        hn = rmsnorm(h, norm_w[layer])
        i = layer // 2
        if layer % 2 == 0:
            h = h + even_mixer(hn, even_w_in[i], even_conv_w[i], even_conv_b[i], even_dt_bias[i],
                               even_a_log[i], even_d_skip[i], even_ssd_norm_w[i],
                               even_sgu_ln_w[i], even_sgu_ln_b[i], even_sgu_ws[i],
                               even_sgu_b[i], even_w_out[i])
        else:
            lambda_init = 0.8 - 0.6 * math.exp(-0.3 * layer)
            h = h + odd_mixer(hn, odd_w_in[i], odd_lam_q1[i], odd_lam_k1[i], odd_lam_q2[i],
                              odd_lam_k2[i], odd_subln_w[i], odd_w_out[i], lambda_init)
    return rmsnorm(h, final_norm_w)
```

```python
import functools
import math

import jax
import jax.numpy as jnp
from jax import lax
from jax.experimental import pallas as pl
from jax.experimental.pallas import tpu as pltpu

F32 = jnp.float32
BF16 = jnp.bfloat16

D_MODEL = 2048
SSD_WIDTH = 2048
SSD_HEAD_DIM = 64
SSD_HEADS = 32
SSD_GROUPS = 8
SSD_HEADS_PER_GROUP = SSD_HEADS // SSD_GROUPS
SSD_GROUP_WIDTH = SSD_WIDTH // SSD_GROUPS
SSD_STATE = 128
SSD_CONV = 4
SSD_CHUNK = 128
SSD_BC_WIDTH = SSD_GROUPS * SSD_STATE
SGU_WIDTH = 2048
SGU_CHUNK = 128
SGU_GROUPS = 16
SGU_GROUP_DIM = 128
DIFF_HEADS = 16
DIFF_HEAD_DIM = 128
DIFF_V_DIM = 256
DIFF_WIDTH = DIFF_HEADS * DIFF_V_DIM
EPS = 1e-6

V7X_LANES = 128
V7X_VMEM_BYTES = 64 * 1024 * 1024
DT_PAD = V7X_LANES


def _vmem_limit(block_bytes, scratch_bytes=0, temp_bytes=0):
    est = 2 * block_bytes + scratch_bytes + temp_bytes + (4 << 20)
    return int(min(est, V7X_VMEM_BYTES - (8 << 20)))


def _nbytes(shape, dtype):
    return math.prod(shape) * jnp.dtype(dtype).itemsize


def _silu(x):
    return x * (1.0 / (1.0 + jnp.exp(-x)))


def _gelu_tanh(x):
    c = math.sqrt(2.0 / math.pi)
    return x * (0.5 * (1.0 + jnp.tanh(c * (x + 0.044715 * (x * x * x)))))


def _split_bf16(x, n):
    parts = []
    r = x
    for i in range(n):
        p = r.astype(BF16)
        parts.append(p)
        if i + 1 < n:
            r = r - p.astype(F32)
    return parts


def _dot(a, b):
    return jnp.dot(a, b, preferred_element_type=F32)


def _dot_nt(a, b):
    return lax.dot_general(a, b, (((1,), (1,)), ((), ())), preferred_element_type=F32)


def _rms_kernel(x_ref, w_ref, o_ref):
    x = x_ref[...].astype(F32)
    ms = jnp.mean(x * x, axis=-1, keepdims=True)
    o_ref[...] = (x * lax.rsqrt(ms + EPS) * w_ref[...]).astype(o_ref.dtype)


def _rms_dt_kernel(x_ref, w_ref, wdt_ref, o_ref, dt_ref):
    x = x_ref[...]
    ms = jnp.mean(x * x, axis=-1, keepdims=True)
    y = x * lax.rsqrt(ms + EPS) * w_ref[...]
    y_hi, y_lo = _split_bf16(y, 2)
    o_ref[...] = y_hi
    w_hi, w_lo = _split_bf16(wdt_ref[...], 2)
    dt_ref[...] = _dot(y_hi, w_hi) + (_dot(y_lo, w_hi) + _dot(y_hi, w_lo))


def _rmsnorm(x, w, out_dtype, bm=256):
    m, d = x.shape
    blk = _nbytes((bm, d), x.dtype) + _nbytes((bm, d), out_dtype)
    return pl.pallas_call(
        _rms_kernel,
        grid=(m // bm,),
        in_specs=[pl.BlockSpec((bm, d), lambda i: (i, 0)),
                  pl.BlockSpec((1, d), lambda i: (0, 0))],
        out_specs=pl.BlockSpec((bm, d), lambda i: (i, 0)),
        out_shape=jax.ShapeDtypeStruct((m, d), out_dtype),
        compiler_params=pltpu.CompilerParams(
            dimension_semantics=("arbitrary",),
            vmem_limit_bytes=_vmem_limit(blk, temp_bytes=3 * _nbytes((bm, d), F32))),
        name="rmsnorm",
    )(x, w.reshape(1, d))


def _rmsnorm_dt(x, w, wdt, bm=256):
    m, d = x.shape
    blk = _nbytes((bm, d), F32) + _nbytes((bm, d), BF16) + _nbytes((d, DT_PAD), F32)
    return pl.pallas_call(
        _rms_dt_kernel,
        grid=(m // bm,),
        in_specs=[pl.BlockSpec((bm, d), lambda i: (i, 0)),
                  pl.BlockSpec((1, d), lambda i: (0, 0)),
                  pl.BlockSpec((d, DT_PAD), lambda i: (0, 0))],
        out_specs=[pl.BlockSpec((bm, d), lambda i: (i, 0)),
                   pl.BlockSpec((bm, DT_PAD), lambda i: (i, 0))],
        out_shape=[jax.ShapeDtypeStruct((m, d), BF16),
                   jax.ShapeDtypeStruct((m, DT_PAD), F32)],
        compiler_params=pltpu.CompilerParams(
            dimension_semantics=("arbitrary",),
            vmem_limit_bytes=_vmem_limit(blk, temp_bytes=4 * _nbytes((bm, d), F32))),
        name="rmsnorm_dt",
    )(x, w.reshape(1, d), wdt)


def _mm_kernel(x_ref, w_ref, o_ref):
    o_ref[...] = _dot(x_ref[...], w_ref[...]).astype(o_ref.dtype)


def _matmul(x, w, out_dtype, bm=1024, bn=1024):
    m, k = x.shape
    _, n = w.shape
    blk = _nbytes((bm, k), x.dtype) + _nbytes((k, bn), w.dtype) + _nbytes((bm, bn), out_dtype)
    return pl.pallas_call(
        _mm_kernel,
        grid=(m // bm, n // bn),
        in_specs=[pl.BlockSpec((bm, k), lambda i, j: (i, 0)),
                  pl.BlockSpec((k, bn), lambda i, j: (0, j))],
        out_specs=pl.BlockSpec((bm, bn), lambda i, j: (i, j)),
        out_shape=jax.ShapeDtypeStruct((m, n), out_dtype),
        compiler_params=pltpu.CompilerParams(
            dimension_semantics=("arbitrary", "arbitrary"),
            vmem_limit_bytes=_vmem_limit(blk, temp_bytes=_nbytes((bm, bn), F32))),
        name="in_proj",
    )(x, w)


def _mm2_res_kernel(xa_ref, xb_ref, wa_ref, wb_ref, r_ref, o_ref):
    acc = _dot(xa_ref[...], wa_ref[...]) + _dot(xb_ref[...], wb_ref[...])
    o_ref[...] = r_ref[...] + acc


def _out_proj_residual(xa, xb_src, xb_col, w, res, bm=1024, bn=512):
    m = xa.shape[0]
    k = w.shape[0] // 2
    n = w.shape[1]
    blk = (2 * _nbytes((bm, k), BF16) + 2 * _nbytes((k, bn), BF16) + 2 * _nbytes((bm, bn), F32))
    return pl.pallas_call(
        _mm2_res_kernel,
        grid=(m // bm, n // bn),
        in_specs=[pl.BlockSpec((bm, k), lambda i, j: (i, 0)),
                  pl.BlockSpec((bm, k), lambda i, j: (i, xb_col)),
                  pl.BlockSpec((k, bn), lambda i, j: (0, j)),
                  pl.BlockSpec((k, bn), lambda i, j: (1, j)),
                  pl.BlockSpec((bm, bn), lambda i, j: (i, j))],
        out_specs=pl.BlockSpec((bm, bn), lambda i, j: (i, j)),
        out_shape=jax.ShapeDtypeStruct((m, n), F32),
        compiler_params=pltpu.CompilerParams(
            dimension_semantics=("arbitrary", "arbitrary"),
            vmem_limit_bytes=_vmem_limit(blk, temp_bytes=2 * _nbytes((bm, bn), F32))),
        name="out_proj",
    )(xa, xb_src, w, w, res)


_CONV_HALO = 8
_CONV_COLS = 512


def _ssd_kernel(z_ref, xs_ref, bc_ref, dtr_ref, cw_ref, cb_ref, dtb_ref, alog_ref, dskip_ref,
                nw_ref, exp_ref, o_ref, xpad_ref, xc_ref, state_ref):
    L = SSD_CHUNK
    c = pl.program_id(1)

    @pl.when(c == 0)
    def _():
        state_ref[...] = jnp.zeros_like(state_ref)
        xpad_ref[0:_CONV_HALO, :] = jnp.zeros((_CONV_HALO, xpad_ref.shape[1]), F32)

    xpad_ref[_CONV_HALO:_CONV_HALO + L, 0:SSD_WIDTH] = xs_ref[...].astype(F32)
    xpad_ref[_CONV_HALO:_CONV_HALO + L, SSD_WIDTH:] = bc_ref[...].astype(F32)
    for j in range(xpad_ref.shape[1] // _CONV_COLS):
        cols = slice(j * _CONV_COLS, (j + 1) * _CONV_COLS)
        acc = cb_ref[:, cols]
        for k in range(SSD_CONV):
            r0 = _CONV_HALO - (SSD_CONV - 1) + k
            acc = acc + cw_ref[k:k + 1, cols] * xpad_ref[r0:r0 + L, cols]
        xc_ref[:, cols] = _silu(acc)
    tail = xpad_ref[_CONV_HALO + L - (SSD_CONV - 1):_CONV_HALO + L, :]
    xpad_ref[_CONV_HALO - (SSD_CONV - 1):_CONV_HALO, :] = tail

    x_dt = dtr_ref[...] + dtb_ref[...]
    dt = jnp.maximum(x_dt, 0.0) + jnp.log1p(jnp.exp(-jnp.abs(x_dt)))
    da = dt * (-jnp.exp(alog_ref[...]))
    row = lax.broadcasted_iota(jnp.int32, (L, L), 0)
    col = lax.broadcasted_iota(jnp.int32, (L, L), 1)
    causal = col <= row
    tri = jnp.where(causal, 1.0, 0.0).astype(BF16)
    cs = sum(_dot(tri, p) for p in _split_bf16(da, 3))
    cs_t = cs.T
    cs_last = cs[L - 1:L, :]
    exp_cs = jnp.exp(cs)
    w_state = jnp.exp(cs_last - cs) * dt
    chunk_decay = jnp.broadcast_to(jnp.exp(cs_last), (8, DT_PAD))

    def expand(a):
        return sum(_dot(p, exp_ref[...]) for p in _split_bf16(a, 2))

    dt_x = expand(dt)
    w_state_x = expand(w_state)
    chunk_decay_x = expand(chunk_decay)[0:1, :]

    lane = lax.broadcasted_iota(jnp.int32, (2 * L, SSD_GROUP_WIDTH), 1)
    for g in range(SSD_GROUPS):
        gcols = slice(g * SSD_GROUP_WIDTH, (g + 1) * SSD_GROUP_WIDTH)
        xg = xc_ref[:, gcols]
        b0 = SSD_WIDTH + g * SSD_STATE
        c0 = SSD_WIDTH + SSD_BC_WIDTH + g * SSD_STATE
        bg = xc_ref[:, b0:b0 + SSD_STATE].astype(BF16)
        cg = xc_ref[:, c0:c0 + SSD_STATE]
        cbm = jnp.where(causal, _dot_nt(cg.astype(BF16), bg), 0.0)
        prev = state_ref[g]
        rhs_all = jnp.concatenate([(xg * dt_x[:, gcols]).astype(BF16), prev.astype(BF16)], axis=0)
        lhs_parts, rhs_parts = [], []
        for r in range(SSD_HEADS_PER_GROUP):
            h = g * SSD_HEADS_PER_GROUP + r
            seg = jnp.where(causal, cs[:, h:h + 1] - cs_t[h:h + 1, :], 0.0)
            lhs_parts.append((cbm * jnp.exp(seg)).astype(BF16))
            lhs_parts.append((cg * exp_cs[:, h:h + 1]).astype(BF16))
            in_head = (lane >= r * SSD_HEAD_DIM) & (lane < (r + 1) * SSD_HEAD_DIM)
            rhs_parts.append(jnp.where(in_head, rhs_all, jnp.zeros_like(rhs_all)))
        y = _dot(jnp.concatenate(lhs_parts, axis=1), jnp.concatenate(rhs_parts, axis=0))
        y = y + xg * dskip_ref[:, gcols]
        y = y * _silu(z_ref[:, gcols].astype(F32))
        ms = jnp.mean(y * y, axis=-1, keepdims=True)
        o_ref[:, gcols] = (y * lax.rsqrt(ms + EPS) * nw_ref[:, gcols]).astype(o_ref.dtype)
        xs_w = (xg * w_state_x[:, gcols]).astype(BF16)
        state_ref[g] = prev * chunk_decay_x[:, gcols] + _dot(bg.T, xs_w)


def _ssd(proj, dt_raw, conv_w, conv_b, dt_bias, a_log, d_skip, norm_w):
    bsz, s, _ = proj.shape
    L = SSD_CHUNK
    conv_dim = SSD_WIDTH + 2 * SSD_BC_WIDTH
    pad = DT_PAD - SSD_HEADS
    head_of_col = jnp.arange(SSD_WIDTH) // SSD_HEAD_DIM
    expand_mat = (jnp.arange(DT_PAD)[:, None] == head_of_col[None, :]).astype(BF16)
    blk = (3 * _nbytes((L, SSD_WIDTH), BF16) + _nbytes((L, DT_PAD), F32) + _nbytes((L, SSD_WIDTH), BF16)
           + _nbytes((SSD_CONV + 1, conv_dim), F32) + _nbytes((DT_PAD, SSD_WIDTH), BF16))
    scratch = (_nbytes((_CONV_HALO + L, conv_dim), F32) + _nbytes((L, conv_dim), F32)
               + _nbytes((SSD_GROUPS, SSD_STATE, SSD_GROUP_WIDTH), F32))
    row = lambda b, c: (b, c, 0)
    const = lambda b, c: (0, 0)
    return pl.pallas_call(
        _ssd_kernel,
        grid=(bsz, s // L),
        in_specs=[pl.BlockSpec((None, L, SSD_WIDTH), lambda b, c: (b, c, 0)),
                  pl.BlockSpec((None, L, SSD_WIDTH), lambda b, c: (b, c, 1)),
                  pl.BlockSpec((None, L, SSD_WIDTH), lambda b, c: (b, c, 2)),
                  pl.BlockSpec((None, L, DT_PAD), row),
                  pl.BlockSpec((SSD_CONV, conv_dim), const),
                  pl.BlockSpec((1, conv_dim), const),
                  pl.BlockSpec((1, DT_PAD), const),
                  pl.BlockSpec((1, DT_PAD), const),
                  pl.BlockSpec((1, SSD_WIDTH), const),
                  pl.BlockSpec((1, SSD_WIDTH), const),
                  pl.BlockSpec((DT_PAD, SSD_WIDTH), const)],
        out_specs=pl.BlockSpec((None, L, SSD_WIDTH), row),
        out_shape=jax.ShapeDtypeStruct((bsz, s, SSD_WIDTH), BF16),
        scratch_shapes=[pltpu.VMEM((_CONV_HALO + L, conv_dim), F32),
                        pltpu.VMEM((L, conv_dim), F32),
                        pltpu.VMEM((SSD_GROUPS, SSD_STATE, SSD_GROUP_WIDTH), F32)],
        compiler_params=pltpu.CompilerParams(
            dimension_semantics=("arbitrary", "arbitrary"),
            vmem_limit_bytes=_vmem_limit(blk, scratch, temp_bytes=16 << 20)),
        name="ssd",
    )(proj, proj, proj, dt_raw, conv_w, conv_b.reshape(1, conv_dim),
      jnp.pad(dt_bias, (0, pad)).reshape(1, DT_PAD), jnp.pad(a_log, (0, pad)).reshape(1, DT_PAD),
      jnp.repeat(d_skip, SSD_HEAD_DIM).reshape(1, SSD_WIDTH), norm_w.reshape(1, SSD_WIDTH), expand_mat)


_SGU_CHUNKS_PER_STEP = 2


def _sgu_kernel(zb_ref, u_ref, v_ref, lnw_ref, lnb_ref, ws_ref, bias_ref, o_ref):
    T = SGU_CHUNK
    v = _gelu_tanh(v_ref[...].astype(F32))
    mu = jnp.mean(v, axis=-1, keepdims=True)
    vc = v - mu
    var = jnp.mean(vc * vc, axis=-1, keepdims=True)
    vn = (vc * lax.rsqrt(var + EPS) * lnw_ref[...] + lnb_ref[...]).astype(BF16)
    row = lax.broadcasted_iota(jnp.int32, (T, T), 0)
    col = lax.broadcasted_iota(jnp.int32, (T, T), 1)
    for g in range(SGU_GROUPS):
        gcols = slice(g * SGU_GROUP_DIM, (g + 1) * SGU_GROUP_DIM)
        ws = jnp.where(col <= row, ws_ref[g], 0.0).astype(BF16)
        rhs = jnp.concatenate([vn[k * T:(k + 1) * T, gcols] for k in range(_SGU_CHUNKS_PER_STEP)], axis=1)
        mixed = _dot(ws, rhs) + bias_ref[:, g:g + 1]
        for k in range(_SGU_CHUNKS_PER_STEP):
            rows = slice(k * T, (k + 1) * T)
            u = _gelu_tanh(u_ref[rows, gcols].astype(F32))
            gate = _silu(zb_ref[rows, gcols].astype(F32))
            o_ref[rows, gcols] = (u * mixed[:, k * SGU_GROUP_DIM:(k + 1) * SGU_GROUP_DIM] * gate).astype(o_ref.dtype)


def _sgu(proj2d, col0, ln_w, ln_b, ws, bias):
    m = proj2d.shape[0]
    bm = _SGU_CHUNKS_PER_STEP * SGU_CHUNK
    blk = 4 * _nbytes((bm, SGU_WIDTH), BF16) + _nbytes((SGU_GROUPS, SGU_CHUNK, SGU_CHUNK), F32)
    const2 = lambda i: (0, 0)
    return pl.pallas_call(
        _sgu_kernel,
        grid=(m // bm,),
        in_specs=[pl.BlockSpec((bm, SGU_WIDTH), lambda i: (i, col0)),
                  pl.BlockSpec((bm, SGU_WIDTH), lambda i: (i, col0 + 1)),
                  pl.BlockSpec((bm, SGU_WIDTH), lambda i: (i, col0 + 2)),
                  pl.BlockSpec((1, SGU_WIDTH), const2),
                  pl.BlockSpec((1, SGU_WIDTH), const2),
                  pl.BlockSpec((SGU_GROUPS, SGU_CHUNK, SGU_CHUNK), lambda i: (0, 0, 0)),
                  pl.BlockSpec((SGU_CHUNK, SGU_GROUPS), const2)],
        out_specs=pl.BlockSpec((bm, SGU_WIDTH), lambda i: (i, 0)),
        out_shape=jax.ShapeDtypeStruct((m, SGU_WIDTH), BF16),
        compiler_params=pltpu.CompilerParams(
            dimension_semantics=("arbitrary",),
            vmem_limit_bytes=_vmem_limit(blk, temp_bytes=6 * _nbytes((bm, SGU_WIDTH), F32))),
        name="sgu",
    )(proj2d, proj2d, proj2d, ln_w.reshape(1, SGU_WIDTH), ln_b.reshape(1, SGU_WIDTH), ws, bias.T)


_ATTN_BQ = 256


def _attn_kernel(q_ref, k_ref, v_ref, g_ref, lq1_ref, lk1_ref, lq2_ref, lk2_ref, sw_ref, o_ref, *,
                 lambda_init):
    s_len = q_ref.shape[0]
    d = DIFF_HEAD_DIM
    bq = _ATTN_BQ
    scale = d ** -0.5
    lam = (jnp.exp(jnp.sum(lq1_ref[...] * lk1_ref[...], axis=-1, keepdims=True))
           - jnp.exp(jnp.sum(lq2_ref[...] * lk2_ref[...], axis=-1, keepdims=True)) + lambda_init)
    row = lax.broadcasted_iota(jnp.int32, (bq, bq), 0)
    col = lax.broadcasted_iota(jnp.int32, (bq, bq), 1)
    causal = col <= row
    for i in range(s_len // bq):
        lo, hi = i * bq, (i + 1) * bq
        q = q_ref[lo:hi, :]
        exps, sums = [], []
        for j in range(2):
            qj = q[:, j * d:(j + 1) * d]
            s_diag = jnp.where(causal, _dot_nt(qj, k_ref[lo:hi, j * d:(j + 1) * d]), -jnp.inf)
            m = jnp.max(s_diag, axis=-1, keepdims=True)
            if i > 0:
                s_past = _dot_nt(qj, k_ref[0:lo, j * d:(j + 1) * d])
                m = jnp.maximum(m, jnp.max(s_past, axis=-1, keepdims=True))
                e_past = jnp.exp((s_past - m) * scale)
            e_diag = jnp.exp((s_diag - m) * scale)
            total = jnp.sum(e_diag, axis=-1, keepdims=True)
            if i > 0:
                total = total + jnp.sum(e_past, axis=-1, keepdims=True)
                exps.append((e_past, e_diag))
            else:
                exps.append((None, e_diag))
            sums.append(total)
        c1 = 1.0 / sums[0]
        c2 = lam / sums[1]
        p_diag = (exps[0][1] * c1 - exps[1][1] * c2).astype(BF16)
        o = _dot(p_diag, v_ref[lo:hi, :])
        if i > 0:
            p_past = (exps[0][0] * c1 - exps[1][0] * c2).astype(BF16)
            o = o + _dot(p_past, v_ref[0:lo, :])
        ms = jnp.mean(o * o, axis=-1, keepdims=True)
        o = o * lax.rsqrt(ms + EPS) * sw_ref[...] * (1.0 - lambda_init)
        o_ref[lo:hi, :] = (o * _silu(g_ref[lo:hi, :].astype(F32))).astype(o_ref.dtype)


def _diff_attention(proj, lq1, lk1, lq2, lk2, subln_w, lambda_init):
    bsz, s, _ = proj.shape
    blk = 5 * _nbytes((s, DIFF_V_DIM), BF16)
    vec = lambda a: a.reshape(1, -1)
    const = lambda b, h: (0, 0)
    return pl.pallas_call(
        functools.partial(_attn_kernel, lambda_init=lambda_init),
        grid=(bsz, DIFF_HEADS),
        in_specs=[pl.BlockSpec((None, s, DIFF_V_DIM), lambda b, h: (b, 0, h)),
                  pl.BlockSpec((None, s, DIFF_V_DIM), lambda b, h: (b, 0, DIFF_HEADS + h)),
                  pl.BlockSpec((None, s, DIFF_V_DIM), lambda b, h: (b, 0, 2 * DIFF_HEADS + h)),
                  pl.BlockSpec((None, s, DIFF_V_DIM), lambda b, h: (b, 0, 3 * DIFF_HEADS + h)),
                  pl.BlockSpec((1, DIFF_HEAD_DIM), const),
                  pl.BlockSpec((1, DIFF_HEAD_DIM), const),
                  pl.BlockSpec((1, DIFF_HEAD_DIM), const),
                  pl.BlockSpec((1, DIFF_HEAD_DIM), const),
                  pl.BlockSpec((1, DIFF_V_DIM), const)],
        out_specs=pl.BlockSpec((None, s, DIFF_V_DIM), lambda b, h: (b, 0, h)),
        out_shape=jax.ShapeDtypeStruct((bsz, s, DIFF_WIDTH), BF16),
        compiler_params=pltpu.CompilerParams(
            dimension_semantics=("arbitrary", "arbitrary"),
            vmem_limit_bytes=_vmem_limit(blk, temp_bytes=8 * _nbytes((_ATTN_BQ, s), F32))),
        name="diff_attn",
    )(proj, proj, proj, proj, vec(lq1), vec(lk1), vec(lq2), vec(lk2), vec(subln_w))


def kernel(x, norm_w, even_w_in, even_conv_w, even_conv_b, even_dt_bias, even_a_log, even_d_skip,
           even_ssd_norm_w, even_sgu_ln_w, even_sgu_ln_b, even_sgu_ws, even_sgu_b, even_w_out,
           odd_w_in, odd_lam_q1, odd_lam_k1, odd_lam_q2, odd_lam_k2, odd_subln_w, odd_w_out,
           final_norm_w):
    bsz, s, d = x.shape
    m = bsz * s
    x2 = x.reshape(m, d)

    w_in = even_w_in[0]
    dt_lo = 2 * SSD_WIDTH + 2 * SSD_BC_WIDTH
    dt_hi = dt_lo + SSD_HEADS
    w_main = jnp.concatenate([w_in[:, :dt_lo], w_in[:, dt_hi:]], axis=1).astype(BF16)
    w_dt = jnp.pad(w_in[:, dt_lo:dt_hi], ((0, 0), (0, DT_PAD - SSD_HEADS)))
    hn, dt_raw = _rmsnorm_dt(x2, norm_w[0], w_dt)
    proj = _matmul(hn, w_main, BF16)
    y_a = _ssd(proj.reshape(bsz, s, -1), dt_raw.reshape(bsz, s, DT_PAD), even_conv_w[0], even_conv_b[0],
               even_dt_bias[0], even_a_log[0], even_d_skip[0], even_ssd_norm_w[0])
    y_b = _sgu(proj, 3, even_sgu_ln_w[0], even_sgu_ln_b[0], even_sgu_ws[0], even_sgu_b[0])
    h1 = _out_proj_residual(y_a.reshape(m, SSD_WIDTH), y_b, 0, even_w_out[0].astype(BF16), x2)

    lambda_init = 0.8 - 0.6 * math.exp(-0.3 * 1)
    hn1 = _rmsnorm(h1, norm_w[1], BF16)
    proj1 = _matmul(hn1, odd_w_in[0].astype(BF16), BF16)
    o = _diff_attention(proj1.reshape(bsz, s, -1), odd_lam_q1[0], odd_lam_k1[0], odd_lam_q2[0],
                        odd_lam_k2[0], odd_subln_w[0], lambda_init)
    o2 = o.reshape(m, DIFF_WIDTH)
    h2 = _out_proj_residual(o2, o2, 1, odd_w_out[0].astype(BF16), h1)

    return _rmsnorm(h2, final_norm_w, F32).reshape(bsz, s, d)
```

```python
import functools
import math

import jax
import jax.numpy as jnp
from jax import lax
from jax.experimental import pallas as pl
from jax.experimental.pallas import tpu as pltpu

F32 = jnp.float32
BF16 = jnp.bfloat16

D_MODEL = 2048
SSD_WIDTH = 2048
SSD_HEAD_DIM = 64
SSD_HEADS = 32
SSD_GROUPS = 8
SSD_HEADS_PER_GROUP = SSD_HEADS // SSD_GROUPS
SSD_GROUP_WIDTH = SSD_WIDTH // SSD_GROUPS
SSD_STATE = 128
SSD_CONV = 4
SSD_CHUNK = 128
SSD_BC_WIDTH = SSD_GROUPS * SSD_STATE
SSD_DT_COL = 2 * SSD_WIDTH + 2 * SSD_BC_WIDTH
SGU_WIDTH = 2048
SGU_CHUNK = 128
SGU_GROUPS = 16
SGU_GROUP_DIM = 128
DIFF_HEADS = 16
DIFF_HEAD_DIM = 128
DIFF_V_DIM = 256
DIFF_WIDTH = DIFF_HEADS * DIFF_V_DIM
EPS = 1e-6

V7X_LANES = 128
V7X_VMEM_BYTES = 64 * 1024 * 1024
DT_PAD = V7X_LANES


def _vmem_limit(block_bytes, scratch_bytes=0, temp_bytes=0):
    est = 2 * block_bytes + scratch_bytes + temp_bytes + (4 << 20)
    return int(min(est, V7X_VMEM_BYTES - (8 << 20)))


def _nbytes(shape, dtype):
    return math.prod(shape) * jnp.dtype(dtype).itemsize


def _silu(x):
    return x * (1.0 / (1.0 + jnp.exp(-x)))


def _gelu_tanh(x):
    c = math.sqrt(2.0 / math.pi)
    return x * (0.5 * (1.0 + jnp.tanh(c * (x + 0.044715 * (x * x * x)))))


def _split_bf16(x, n):
    parts = []
    r = x
    for i in range(n):
        p = r.astype(BF16)
        parts.append(p)
        if i + 1 < n:
            r = r - p.astype(F32)
    return parts


def _dot(a, b):
    return jnp.dot(a, b, preferred_element_type=F32)


def _dot_nt(a, b):
    return lax.dot_general(a, b, (((1,), (1,)), ((), ())), preferred_element_type=F32)


def _rms_dt_kernel(x_ref, w_ref, wdt_ref, o_ref, dt_ref):
    x = x_ref[...]
    ms = jnp.mean(x * x, axis=-1, keepdims=True)
    y = x * lax.rsqrt(ms + EPS) * w_ref[...]
    y_hi, y_lo = _split_bf16(y, 2)
    o_ref[...] = y_hi
    lane = lax.broadcasted_iota(jnp.int32, wdt_ref.shape, 1)
    w_hi, w_lo = _split_bf16(jnp.where(lane < SSD_HEADS, wdt_ref[...], 0.0), 2)
    dt_ref[...] = _dot(y_hi, w_hi) + (_dot(y_lo, w_hi) + _dot(y_hi, w_lo))


def _rmsnorm_dt(x, w, w_in3, bm=256):
    m, d = x.shape
    blk = _nbytes((bm, d), F32) + _nbytes((bm, d), BF16) + _nbytes((d, DT_PAD), F32)
    return pl.pallas_call(
        _rms_dt_kernel,
        grid=(m // bm,),
        in_specs=[pl.BlockSpec((bm, d), lambda i: (i, 0)),
                  pl.BlockSpec((1, d), lambda i: (0, 0)),
                  pl.BlockSpec((None, d, DT_PAD), lambda i: (0, 0, SSD_DT_COL // DT_PAD))],
        out_specs=[pl.BlockSpec((bm, d), lambda i: (i, 0)),
                   pl.BlockSpec((bm, DT_PAD), lambda i: (i, 0))],
        out_shape=[jax.ShapeDtypeStruct((m, d), BF16),
                   jax.ShapeDtypeStruct((m, DT_PAD), F32)],
        compiler_params=pltpu.CompilerParams(
            dimension_semantics=("arbitrary",),
            vmem_limit_bytes=_vmem_limit(blk, temp_bytes=4 * _nbytes((bm, d), F32))),
        name="rmsnorm_dt",
    )(x, w.reshape(1, d), w_in3)


_CAST_ROWS = 256


def _in_proj_kernel(x_ref, w_ref, *rest, shift_from, shift, scaled):
    rest = list(rest)
    wn_ref = rest.pop(0) if shift else None
    scale_ref = rest.pop(0) if scaled else None
    o_ref, wbf_ref = rest
    k = wbf_ref.shape[0]
    j = pl.program_id(0)
    first = pl.program_id(1) == 0

    def cast_plain():
        for r in range(0, k, _CAST_ROWS):
            wbf_ref[r:r + _CAST_ROWS, :] = w_ref[r:r + _CAST_ROWS, :].astype(BF16)

    def cast_shifted():
        for r in range(0, k, _CAST_ROWS):
            w = jnp.concatenate([w_ref[r:r + _CAST_ROWS, shift:], wn_ref[r:r + _CAST_ROWS, :shift]], axis=1)
            wbf_ref[r:r + _CAST_ROWS, :] = w.astype(BF16)

    if shift:
        pl.when(jnp.logical_and(first, j < shift_from))(cast_plain)
        pl.when(jnp.logical_and(first, j >= shift_from))(cast_shifted)
    else:
        pl.when(first)(cast_plain)
    acc = _dot(x_ref[...], wbf_ref[...])
    if scaled:
        acc = acc * scale_ref[...]
    o_ref[...] = acc.astype(o_ref.dtype)


def _in_proj(x, w3, n_out, skip_col=None, skip=0, col_scale=None, bm=1024, bn=1024):
    m, k = x.shape
    in_specs = [pl.BlockSpec((bm, k), lambda j, i: (i, 0)),
                pl.BlockSpec((None, k, bn), lambda j, i: (0, 0, j))]
    args = [x, w3]
    blk = _nbytes((bm, k), BF16) + _nbytes((k, bn), F32) + _nbytes((bm, bn), BF16)
    shift_from = None
    if skip:
        assert skip_col % bn == 0 and 0 < skip < V7X_LANES
        shift_from = skip_col // bn
        per = bn // V7X_LANES
        in_specs.append(pl.BlockSpec((None, k, V7X_LANES), lambda j, i: (0, 0, (j + 1) * per)))
        args.append(w3)
        blk += _nbytes((k, V7X_LANES), F32)
    if col_scale is not None:
        in_specs.append(pl.BlockSpec((1, bn), lambda j, i: (0, j)))
        args.append(col_scale.reshape(1, n_out))
    return pl.pallas_call(
        functools.partial(_in_proj_kernel, shift_from=shift_from, shift=skip, scaled=col_scale is not None),
        grid=(n_out // bn, m // bm),
        in_specs=in_specs,
        out_specs=pl.BlockSpec((bm, bn), lambda j, i: (i, j)),
        out_shape=jax.ShapeDtypeStruct((m, n_out), BF16),
        scratch_shapes=[pltpu.VMEM((k, bn), BF16)],
        compiler_params=pltpu.CompilerParams(
            dimension_semantics=("arbitrary", "arbitrary"),
            vmem_limit_bytes=_vmem_limit(blk, _nbytes((k, bn), BF16), temp_bytes=_nbytes((bm, bn), F32))),
        name="in_proj",
    )(*args)


def _out_proj_kernel(xa_ref, xb_ref, wa_ref, wb_ref, r_ref, nw_ref, *out_refs):
    h = r_ref[...] + (_dot(xa_ref[...], wa_ref[...]) + _dot(xb_ref[...], wb_ref[...]))
    if len(out_refs) == 2:
        out_refs[0][...] = h
    hn_ref = out_refs[-1]
    ms = jnp.mean(h * h, axis=-1, keepdims=True)
    hn_ref[...] = (h * lax.rsqrt(ms + EPS) * nw_ref[...]).astype(hn_ref.dtype)


def _out_proj(xa, xb_src, xb_col, w, res, norm_w, norm_dtype, keep_residual, bm=512):
    m = xa.shape[0]
    k = w.shape[0] // 2
    n = w.shape[1]
    blk = 2 * _nbytes((bm, k), BF16) + 2 * _nbytes((bm, n), F32) + _nbytes((bm, n), norm_dtype)
    row = lambda i: (i, 0)
    once = pl.Buffered(1)
    out_specs = [pl.BlockSpec((bm, n), row)]
    out_shape = [jax.ShapeDtypeStruct((m, n), norm_dtype)]
    if keep_residual:
        out_specs.insert(0, pl.BlockSpec((bm, n), row))
        out_shape.insert(0, jax.ShapeDtypeStruct((m, n), F32))
    return pl.pallas_call(
        _out_proj_kernel,
        grid=(m // bm,),
        in_specs=[pl.BlockSpec((bm, k), row),
                  pl.BlockSpec((bm, k), lambda i: (i, xb_col)),
                  pl.BlockSpec((k, n), lambda i: (0, 0), pipeline_mode=once),
                  pl.BlockSpec((k, n), lambda i: (1, 0), pipeline_mode=once),
                  pl.BlockSpec((bm, n), row),
                  pl.BlockSpec((1, n), lambda i: (0, 0))],
        out_specs=out_specs,
        out_shape=out_shape,
        compiler_params=pltpu.CompilerParams(
            dimension_semantics=("arbitrary",),
            vmem_limit_bytes=_vmem_limit(blk, 2 * _nbytes((k, n), BF16), temp_bytes=3 * _nbytes((bm, n), F32))),
        name="out_proj",
    )(xa, xb_src, w, w, res, norm_w.reshape(1, n))


_CONV_HALO = 8
_CONV_COLS = 512


def _ssd_kernel(z_ref, xs_ref, bc_ref, dtr_ref, cw_ref, cb_ref, dtb_ref, alog_ref, dskip_ref,
                nw_ref, exp_ref, o_ref, xpad_ref, xc_ref, state_ref):
    L = SSD_CHUNK
    c = pl.program_id(1)

    @pl.when(c == 0)
    def _():
        state_ref[...] = jnp.zeros_like(state_ref)
        xpad_ref[0:_CONV_HALO, :] = jnp.zeros((_CONV_HALO, xpad_ref.shape[1]), F32)

    xpad_ref[_CONV_HALO:_CONV_HALO + L, 0:SSD_WIDTH] = xs_ref[...].astype(F32)
    xpad_ref[_CONV_HALO:_CONV_HALO + L, SSD_WIDTH:] = bc_ref[...].astype(F32)
    for j in range(xpad_ref.shape[1] // _CONV_COLS):
        cols = slice(j * _CONV_COLS, (j + 1) * _CONV_COLS)
        acc = cb_ref[:, cols]
        for k in range(SSD_CONV):
            r0 = _CONV_HALO - (SSD_CONV - 1) + k
            acc = acc + cw_ref[k:k + 1, cols] * xpad_ref[r0:r0 + L, cols]
        xc_ref[:, cols] = _silu(acc)
    tail = xpad_ref[_CONV_HALO + L - (SSD_CONV - 1):_CONV_HALO + L, :]
    xpad_ref[_CONV_HALO - (SSD_CONV - 1):_CONV_HALO, :] = tail

    x_dt = dtr_ref[...] + dtb_ref[...]
    dt = jnp.maximum(x_dt, 0.0) + jnp.log1p(jnp.exp(-jnp.abs(x_dt)))
    da = dt * (-jnp.exp(alog_ref[...]))
    row = lax.broadcasted_iota(jnp.int32, (L, L), 0)
    col = lax.broadcasted_iota(jnp.int32, (L, L), 1)
    causal = col <= row
    tri = jnp.where(causal, 1.0, 0.0).astype(BF16)
    cs = sum(_dot(tri, p) for p in _split_bf16(da, 3))
    cs_t = cs.T
    cs_last = cs[L - 1:L, :]
    exp_cs = jnp.exp(cs)
    w_state = jnp.exp(cs_last - cs) * dt
    chunk_decay = jnp.broadcast_to(jnp.exp(cs_last), (8, DT_PAD))

    def expand(a):
        return sum(_dot(p, exp_ref[...]) for p in _split_bf16(a, 2))

    dt_x = expand(dt)
    w_state_x = expand(w_state)
    chunk_decay_x = expand(chunk_decay)[0:1, :]

    lane = lax.broadcasted_iota(jnp.int32, (2 * L, SSD_GROUP_WIDTH), 1)
    for g in range(SSD_GROUPS):
        gcols = slice(g * SSD_GROUP_WIDTH, (g + 1) * SSD_GROUP_WIDTH)
        xg = xc_ref[:, gcols]
        b0 = SSD_WIDTH + g * SSD_STATE
        c0 = SSD_WIDTH + SSD_BC_WIDTH + g * SSD_STATE
        bg = xc_ref[:, b0:b0 + SSD_STATE].astype(BF16)
        cg = xc_ref[:, c0:c0 + SSD_STATE]
        cbm = jnp.where(causal, _dot_nt(cg.astype(BF16), bg), 0.0)
        prev = state_ref[g]
        rhs_all = jnp.concatenate([(xg * dt_x[:, gcols]).astype(BF16), prev.astype(BF16)], axis=0)
        lhs_parts, rhs_parts = [], []
        for r in range(SSD_HEADS_PER_GROUP):
            h = g * SSD_HEADS_PER_GROUP + r
            seg = jnp.where(causal, cs[:, h:h + 1] - cs_t[h:h + 1, :], 0.0)
            lhs_parts.append((cbm * jnp.exp(seg)).astype(BF16))
            lhs_parts.append((cg * exp_cs[:, h:h + 1]).astype(BF16))
            in_head = (lane >= r * SSD_HEAD_DIM) & (lane < (r + 1) * SSD_HEAD_DIM)
            rhs_parts.append(jnp.where(in_head, rhs_all, jnp.zeros_like(rhs_all)))
        y = _dot(jnp.concatenate(lhs_parts, axis=1), jnp.concatenate(rhs_parts, axis=0))
        y = y + xg * dskip_ref[:, gcols]
        y = y * _silu(z_ref[:, gcols].astype(F32))
        ms = jnp.mean(y * y, axis=-1, keepdims=True)
        o_ref[:, gcols] = (y * lax.rsqrt(ms + EPS) * nw_ref[:, gcols]).astype(o_ref.dtype)
        xs_w = (xg * w_state_x[:, gcols]).astype(BF16)
        state_ref[g] = prev * chunk_decay_x[:, gcols] + _dot(bg.T, xs_w)


def _ssd(proj, dt_raw, conv_w, conv_b, dt_bias, a_log, d_skip, norm_w):
    bsz, s, _ = proj.shape
    L = SSD_CHUNK
    conv_dim = SSD_WIDTH + 2 * SSD_BC_WIDTH
    pad = DT_PAD - SSD_HEADS
    head_of_col = jnp.arange(SSD_WIDTH) // SSD_HEAD_DIM
    expand_mat = (jnp.arange(DT_PAD)[:, None] == head_of_col[None, :]).astype(BF16)
    blk = (3 * _nbytes((L, SSD_WIDTH), BF16) + _nbytes((L, DT_PAD), F32) + _nbytes((L, SSD_WIDTH), BF16)
           + _nbytes((SSD_CONV + 1, conv_dim), F32) + _nbytes((DT_PAD, SSD_WIDTH), BF16))
    scratch = (_nbytes((_CONV_HALO + L, conv_dim), F32) + _nbytes((L, conv_dim), F32)
               + _nbytes((SSD_GROUPS, SSD_STATE, SSD_GROUP_WIDTH), F32))
    row = lambda b, c: (b, c, 0)
    const = lambda b, c: (0, 0)
    return pl.pallas_call(
        _ssd_kernel,
        grid=(bsz, s // L),
        in_specs=[pl.BlockSpec((None, L, SSD_WIDTH), lambda b, c: (b, c, 0)),
                  pl.BlockSpec((None, L, SSD_WIDTH), lambda b, c: (b, c, 1)),
                  pl.BlockSpec((None, L, SSD_WIDTH), lambda b, c: (b, c, 2)),
                  pl.BlockSpec((None, L, DT_PAD), row),
                  pl.BlockSpec((SSD_CONV, conv_dim), const),
                  pl.BlockSpec((1, conv_dim), const),
                  pl.BlockSpec((1, DT_PAD), const),
                  pl.BlockSpec((1, DT_PAD), const),
                  pl.BlockSpec((1, SSD_WIDTH), const),
                  pl.BlockSpec((1, SSD_WIDTH), const),
                  pl.BlockSpec((DT_PAD, SSD_WIDTH), const)],
        out_specs=pl.BlockSpec((None, L, SSD_WIDTH), row),
        out_shape=jax.ShapeDtypeStruct((bsz, s, SSD_WIDTH), BF16),
        scratch_shapes=[pltpu.VMEM((_CONV_HALO + L, conv_dim), F32),
                        pltpu.VMEM((L, conv_dim), F32),
                        pltpu.VMEM((SSD_GROUPS, SSD_STATE, SSD_GROUP_WIDTH), F32)],
        compiler_params=pltpu.CompilerParams(
            dimension_semantics=("arbitrary", "arbitrary"),
            vmem_limit_bytes=_vmem_limit(blk, scratch, temp_bytes=16 << 20)),
        name="ssd",
    )(proj, proj, proj, dt_raw, conv_w, conv_b.reshape(1, conv_dim),
      jnp.pad(dt_bias, (0, pad)).reshape(1, DT_PAD), jnp.pad(a_log, (0, pad)).reshape(1, DT_PAD),
      jnp.repeat(d_skip, SSD_HEAD_DIM).reshape(1, SSD_WIDTH), norm_w.reshape(1, SSD_WIDTH), expand_mat)


_SGU_CHUNKS_PER_STEP = 2


def _sgu_kernel(zb_ref, u_ref, v_ref, lnw_ref, lnb_ref, ws_ref, bias_ref, o_ref):
    T = SGU_CHUNK
    v = _gelu_tanh(v_ref[...].astype(F32))
    mu = jnp.mean(v, axis=-1, keepdims=True)
    vc = v - mu
    var = jnp.mean(vc * vc, axis=-1, keepdims=True)
    vn = (vc * lax.rsqrt(var + EPS) * lnw_ref[...] + lnb_ref[...]).astype(BF16)
    row = lax.broadcasted_iota(jnp.int32, (T, T), 0)
    col = lax.broadcasted_iota(jnp.int32, (T, T), 1)
    for g in range(SGU_GROUPS):
        gcols = slice(g * SGU_GROUP_DIM, (g + 1) * SGU_GROUP_DIM)
        ws = jnp.where(col <= row, ws_ref[g], 0.0).astype(BF16)
        rhs = jnp.concatenate([vn[k * T:(k + 1) * T, gcols] for k in range(_SGU_CHUNKS_PER_STEP)], axis=1)
        mixed = _dot(ws, rhs) + bias_ref[:, g:g + 1]
        for k in range(_SGU_CHUNKS_PER_STEP):
            rows = slice(k * T, (k + 1) * T)
            u = _gelu_tanh(u_ref[rows, gcols].astype(F32))
            gate = _silu(zb_ref[rows, gcols].astype(F32))
            o_ref[rows, gcols] = (u * mixed[:, k * SGU_GROUP_DIM:(k + 1) * SGU_GROUP_DIM] * gate).astype(o_ref.dtype)


def _sgu(proj2d, col0, ln_w, ln_b, ws, bias):
    m = proj2d.shape[0]
    bm = _SGU_CHUNKS_PER_STEP * SGU_CHUNK
    blk = 4 * _nbytes((bm, SGU_WIDTH), BF16) + _nbytes((SGU_GROUPS, SGU_CHUNK, SGU_CHUNK), F32)
    const2 = lambda i: (0, 0)
    return pl.pallas_call(
        _sgu_kernel,
        grid=(m // bm,),
        in_specs=[pl.BlockSpec((bm, SGU_WIDTH), lambda i: (i, col0)),
                  pl.BlockSpec((bm, SGU_WIDTH), lambda i: (i, col0 + 1)),
                  pl.BlockSpec((bm, SGU_WIDTH), lambda i: (i, col0 + 2)),
                  pl.BlockSpec((1, SGU_WIDTH), const2),
                  pl.BlockSpec((1, SGU_WIDTH), const2),
                  pl.BlockSpec((SGU_GROUPS, SGU_CHUNK, SGU_CHUNK), lambda i: (0, 0, 0)),
                  pl.BlockSpec((SGU_CHUNK, SGU_GROUPS), const2)],
        out_specs=pl.BlockSpec((bm, SGU_WIDTH), lambda i: (i, 0)),
        out_shape=jax.ShapeDtypeStruct((m, SGU_WIDTH), BF16),
        compiler_params=pltpu.CompilerParams(
            dimension_semantics=("arbitrary",),
            vmem_limit_bytes=_vmem_limit(blk, temp_bytes=6 * _nbytes((bm, SGU_WIDTH), F32))),
        name="sgu",
    )(proj2d, proj2d, proj2d, ln_w.reshape(1, SGU_WIDTH), ln_b.reshape(1, SGU_WIDTH), ws, bias.T)


_ATTN_BQ = 256
_ATTN_BK = 256
_MASKED = -1e30


ATTN_Q_SCALE = DIFF_HEAD_DIM ** -0.5 * math.log2(math.e)


def _softmax_block(st):
    m = jnp.max(st, axis=0, keepdims=True)
    e = jnp.exp2(st - m)
    return e, m, jnp.sum(e, axis=0, keepdims=True)


def _attn_kernel(q_ref, k_ref, v_ref, g_ref, lq1_ref, lk1_ref, lq2_ref, lk2_ref, sw_ref, o_ref, *,
                 lambda_init):
    s_len = q_ref.shape[0]
    d = DIFF_HEAD_DIM
    bq, bk = _ATTN_BQ, _ATTN_BK
    lam = (jnp.exp(jnp.sum(lq1_ref[...] * lk1_ref[...], axis=-1, keepdims=True))
           - jnp.exp(jnp.sum(lq2_ref[...] * lk2_ref[...], axis=-1, keepdims=True)) + lambda_init)
    key_idx = lax.broadcasted_iota(jnp.int32, (bk, bq), 0)
    query_idx = lax.broadcasted_iota(jnp.int32, (bk, bq), 1)
    v_t = v_ref[...].astype(F32).T.astype(BF16)
    for i in range(s_len // bq):
        lo, hi = i * bq, (i + 1) * bq
        q = q_ref[lo:hi, :]
        n_blocks = hi // bk
        weights = []
        for j in range(2):
            qj = q[:, j * d:(j + 1) * d]
            blocks = []
            for t in range(n_blocks):
                blk = _dot_nt(k_ref[t * bk:(t + 1) * bk, j * d:(j + 1) * d], qj)
                if (t + 1) * bk - 1 > lo:
                    blk = jnp.where(key_idx + (t * bk - lo) <= query_idx, blk, _MASKED)
                blocks.append(_softmax_block(blk))
            m_all = functools.reduce(jnp.maximum, [b[1] for b in blocks])
            alphas = [jnp.exp2(b[1] - m_all) for b in blocks]
            total = sum(a * b[2] for a, b in zip(alphas, blocks))
            norm = (1.0 / total) if j == 0 else (lam / total)
            weights.append([(b[0], a * norm) for a, b in zip(alphas, blocks)])
        p_t = jnp.concatenate([(e1 * f1 - e2 * f2).astype(BF16)
                               for (e1, f1), (e2, f2) in zip(weights[0], weights[1])], axis=0)
        o = _dot(v_t[:, 0:hi], p_t).T
        ms = jnp.mean(o * o, axis=-1, keepdims=True)
        o = o * lax.rsqrt(ms + EPS) * sw_ref[...] * (1.0 - lambda_init)
        o_ref[lo:hi, :] = (o * _silu(g_ref[lo:hi, :].astype(F32))).astype(o_ref.dtype)


def _diff_attention(proj, lq1, lk1, lq2, lk2, subln_w, lambda_init):
    bsz, s, _ = proj.shape
    blk = 5 * _nbytes((s, DIFF_V_DIM), BF16)
    vec = lambda a: a.reshape(1, -1)
    const = lambda b, h: (0, 0)
    return pl.pallas_call(
        functools.partial(_attn_kernel, lambda_init=lambda_init),
        grid=(bsz, DIFF_HEADS),
        in_specs=[pl.BlockSpec((None, s, DIFF_V_DIM), lambda b, h: (b, 0, h)),
                  pl.BlockSpec((None, s, DIFF_V_DIM), lambda b, h: (b, 0, DIFF_HEADS + h)),
                  pl.BlockSpec((None, s, DIFF_V_DIM), lambda b, h: (b, 0, 2 * DIFF_HEADS + h)),
                  pl.BlockSpec((None, s, DIFF_V_DIM), lambda b, h: (b, 0, 3 * DIFF_HEADS + h)),
                  pl.BlockSpec((1, DIFF_HEAD_DIM), const),
                  pl.BlockSpec((1, DIFF_HEAD_DIM), const),
                  pl.BlockSpec((1, DIFF_HEAD_DIM), const),
                  pl.BlockSpec((1, DIFF_HEAD_DIM), const),
                  pl.BlockSpec((1, DIFF_V_DIM), const)],
        out_specs=pl.BlockSpec((None, s, DIFF_V_DIM), lambda b, h: (b, 0, h)),
        out_shape=jax.ShapeDtypeStruct((bsz, s, DIFF_WIDTH), BF16),
        compiler_params=pltpu.CompilerParams(
            dimension_semantics=("arbitrary", "arbitrary"),
            vmem_limit_bytes=_vmem_limit(blk, temp_bytes=8 * _nbytes((_ATTN_BQ, s), F32))),
        name="diff_attn",
    )(proj, proj, proj, proj, vec(lq1), vec(lk1), vec(lq2), vec(lk2), vec(subln_w))


def kernel(x, norm_w, even_w_in, even_conv_w, even_conv_b, even_dt_bias, even_a_log, even_d_skip,
           even_ssd_norm_w, even_sgu_ln_w, even_sgu_ln_b, even_sgu_ws, even_sgu_b, even_w_out,
           odd_w_in, odd_lam_q1, odd_lam_k1, odd_lam_q2, odd_lam_k2, odd_subln_w, odd_w_out,
           final_norm_w):
    bsz, s, d = x.shape
    m = bsz * s
    x2 = x.reshape(m, d)

    hn, dt_raw = _rmsnorm_dt(x2, norm_w[0], even_w_in)
    n_main = even_w_in.shape[2] - SSD_HEADS
    proj = _in_proj(hn, even_w_in, n_main, SSD_DT_COL, SSD_HEADS)
    y_a = _ssd(proj.reshape(bsz, s, -1), dt_raw.reshape(bsz, s, DT_PAD), even_conv_w[0], even_conv_b[0],
               even_dt_bias[0], even_a_log[0], even_d_skip[0], even_ssd_norm_w[0])
    y_b = _sgu(proj, 3, even_sgu_ln_w[0], even_sgu_ln_b[0], even_sgu_ws[0], even_sgu_b[0])
    h1, hn1 = _out_proj(y_a.reshape(m, SSD_WIDTH), y_b, 0, even_w_out[0].astype(BF16), x2, norm_w[1],
                        BF16, keep_residual=True)

    lambda_init = 0.8 - 0.6 * math.exp(-0.3 * 1)
    n_in1 = odd_w_in.shape[2]
    q_scale = jnp.where(jnp.arange(n_in1) < DIFF_WIDTH, ATTN_Q_SCALE, 1.0).astype(F32)
    proj1 = _in_proj(hn1, odd_w_in, n_in1, col_scale=q_scale)
    o = _diff_attention(proj1.reshape(bsz, s, -1), odd_lam_q1[0], odd_lam_k1[0], odd_lam_q2[0],
                        odd_lam_k2[0], odd_subln_w[0], lambda_init)
    o2 = o.reshape(m, DIFF_WIDTH)
    (out,) = _out_proj(o2, o2, 1, odd_w_out[0].astype(BF16), h1, final_norm_w, F32, keep_residual=False)
    return out.reshape(bsz, s, d)
```

```python
import functools
import math

import jax
import jax.numpy as jnp
from jax import lax
from jax.experimental import pallas as pl
from jax.experimental.pallas import tpu as pltpu

F32 = jnp.float32
BF16 = jnp.bfloat16

D_MODEL = 2048
SSD_WIDTH = 2048
SSD_HEAD_DIM = 64
SSD_HEADS = 32
SSD_GROUPS = 8
SSD_HEADS_PER_GROUP = SSD_HEADS // SSD_GROUPS
SSD_GROUP_WIDTH = SSD_WIDTH // SSD_GROUPS
SSD_STATE = 128
SSD_CONV = 4
SSD_CHUNK = 128
SSD_BC_WIDTH = SSD_GROUPS * SSD_STATE
SSD_DT_COL = 2 * SSD_WIDTH + 2 * SSD_BC_WIDTH
SGU_WIDTH = 2048
SGU_CHUNK = 128
SGU_GROUPS = 16
SGU_GROUP_DIM = 128
DIFF_HEADS = 16
DIFF_HEAD_DIM = 128
DIFF_V_DIM = 256
DIFF_WIDTH = DIFF_HEADS * DIFF_V_DIM
EPS = 1e-6

V7X_LANES = 128
V7X_VMEM_BYTES = 64 * 1024 * 1024
DT_PAD = V7X_LANES


def _vmem_limit(block_bytes, scratch_bytes=0, temp_bytes=0):
    est = 2 * block_bytes + scratch_bytes + temp_bytes + (4 << 20)
    return int(min(est, V7X_VMEM_BYTES - (8 << 20)))


def _nbytes(shape, dtype):
    return math.prod(shape) * jnp.dtype(dtype).itemsize


def _silu(x):
    h = 0.5 * x
    return h + h * jnp.tanh(h)


def _gelu_tanh(x):
    c = math.sqrt(2.0 / math.pi)
    h = 0.5 * x
    return h + h * jnp.tanh(x * (c + (c * 0.044715) * (x * x)))


def _split_bf16(x, n):
    parts = []
    r = x
    for i in range(n):
        p = r.astype(BF16)
        parts.append(p)
        if i + 1 < n:
            r = r - p.astype(F32)
    return parts


def _dot(a, b):
    return jnp.dot(a, b, preferred_element_type=F32)


def _dot_nt(a, b):
    return lax.dot_general(a, b, (((1,), (1,)), ((), ())), preferred_element_type=F32)


def _rms_dt_kernel(x_ref, w_ref, wdt_ref, o_ref, dt_ref):
    x = x_ref[...]
    ms = jnp.mean(x * x, axis=-1, keepdims=True)
    y = x * lax.rsqrt(ms + EPS) * w_ref[...]
    y_hi, y_lo = _split_bf16(y, 2)
    o_ref[...] = y_hi
    row = lax.broadcasted_iota(jnp.int32, wdt_ref.shape, 0)
    w_hi, w_lo = _split_bf16(jnp.where(row < SSD_HEADS, wdt_ref[...], 0.0), 2)
    dt_ref[...] = _dot_nt(y_hi, w_hi) + (_dot_nt(y_lo, w_hi) + _dot_nt(y_hi, w_lo))


def _rmsnorm_dt(x, w, w_in_t, bm=256):
    m, d = x.shape
    blk = _nbytes((bm, d), F32) + _nbytes((bm, d), BF16) + _nbytes((DT_PAD, d), F32)
    return pl.pallas_call(
        _rms_dt_kernel,
        grid=(m // bm,),
        in_specs=[pl.BlockSpec((bm, d), lambda i: (i, 0)),
                  pl.BlockSpec((1, d), lambda i: (0, 0)),
                  pl.BlockSpec((None, DT_PAD, d), lambda i: (0, SSD_DT_COL // DT_PAD, 0))],
        out_specs=[pl.BlockSpec((bm, d), lambda i: (i, 0)),
                   pl.BlockSpec((bm, DT_PAD), lambda i: (i, 0))],
        out_shape=[jax.ShapeDtypeStruct((m, d), BF16),
                   jax.ShapeDtypeStruct((m, DT_PAD), F32)],
        compiler_params=pltpu.CompilerParams(
            dimension_semantics=("arbitrary",),
            vmem_limit_bytes=_vmem_limit(blk, temp_bytes=4 * _nbytes((bm, d), F32))),
        name="rmsnorm_dt",
    )(x, w.reshape(1, d), w_in_t)


_CAST_ROWS = 256


def _in_proj_kernel(x_ref, w_ref, *rest, transposed, scaled):
    rest = list(rest)
    scale_ref = rest.pop(0) if scaled else None
    o_ref, wbf_ref = rest

    @pl.when(pl.program_id(1) == 0)
    def _():
        for r in range(0, wbf_ref.shape[0], _CAST_ROWS):
            wbf_ref[r:r + _CAST_ROWS, :] = w_ref[r:r + _CAST_ROWS, :].astype(BF16)

    acc = (_dot_nt if transposed else _dot)(x_ref[...], wbf_ref[...])
    if scaled:
        acc = acc * scale_ref[...]
    o_ref[...] = acc.astype(o_ref.dtype)


def _in_proj(x, w3, n_out, transposed=False, skip_col=None, skip=0, col_scale=None, bm=2048, bn=1024):
    m, k = x.shape
    if transposed:
        assert skip % 8 == 0 and (skip == 0 or skip_col % bn == 0)
        start = lambda j: 8 * (j * (bn // 8) + (jnp.where(j * bn >= skip_col, skip // 8, 0) if skip else 0))
        w_spec = pl.BlockSpec((None, pl.Element(bn), pl.Element(k)), lambda j, i: (0, start(j), 0))
        w_block = (bn, k)
    else:
        assert not skip
        w_spec = pl.BlockSpec((None, k, bn), lambda j, i: (0, 0, j))
        w_block = (k, bn)
    in_specs = [pl.BlockSpec((bm, k), lambda j, i: (i, 0)), w_spec]
    args = [x, w3]
    blk = _nbytes((bm, k), BF16) + _nbytes(w_block, F32) + _nbytes((bm, bn), BF16)
    if col_scale is not None:
        in_specs.append(pl.BlockSpec((1, bn), lambda j, i: (0, j)))
        args.append(col_scale.reshape(1, n_out))
    return pl.pallas_call(
        functools.partial(_in_proj_kernel, transposed=transposed, scaled=col_scale is not None),
        grid=(n_out // bn, m // bm),
        in_specs=in_specs,
        out_specs=pl.BlockSpec((bm, bn), lambda j, i: (i, j)),
        out_shape=jax.ShapeDtypeStruct((m, n_out), BF16),
        scratch_shapes=[pltpu.VMEM(w_block, BF16)],
        compiler_params=pltpu.CompilerParams(
            dimension_semantics=("arbitrary", "arbitrary"),
            vmem_limit_bytes=_vmem_limit(blk, _nbytes(w_block, BF16), temp_bytes=_nbytes((bm, bn), F32))),
        name="in_proj",
    )(*args)


def _out_proj_kernel(xa_ref, xb_ref, wa_ref, wb_ref, r_ref, nw_ref, *out_refs):
    h = r_ref[...] + (_dot(xa_ref[...], wa_ref[...]) + _dot(xb_ref[...], wb_ref[...]))
    if len(out_refs) == 2:
        out_refs[0][...] = h
    hn_ref = out_refs[-1]
    ms = jnp.mean(h * h, axis=-1, keepdims=True)
    hn_ref[...] = (h * lax.rsqrt(ms + EPS) * nw_ref[...]).astype(hn_ref.dtype)


def _out_proj(xa, xb_src, xb_col, w, res, norm_w, norm_dtype, keep_residual, bm=512):
    m = xa.shape[0]
    k = w.shape[0] // 2
    n = w.shape[1]
    blk = 2 * _nbytes((bm, k), BF16) + 2 * _nbytes((bm, n), F32) + _nbytes((bm, n), norm_dtype)
    row = lambda i: (i, 0)
    once = pl.Buffered(1)
    out_specs = [pl.BlockSpec((bm, n), row)]
    out_shape = [jax.ShapeDtypeStruct((m, n), norm_dtype)]
    if keep_residual:
        out_specs.insert(0, pl.BlockSpec((bm, n), row))
        out_shape.insert(0, jax.ShapeDtypeStruct((m, n), F32))
    return pl.pallas_call(
        _out_proj_kernel,
        grid=(m // bm,),
        in_specs=[pl.BlockSpec((bm, k), row),
                  pl.BlockSpec((bm, k), lambda i: (i, xb_col)),
                  pl.BlockSpec((k, n), lambda i: (0, 0), pipeline_mode=once),
                  pl.BlockSpec((k, n), lambda i: (1, 0), pipeline_mode=once),
                  pl.BlockSpec((bm, n), row),
                  pl.BlockSpec((1, n), lambda i: (0, 0))],
        out_specs=out_specs,
        out_shape=out_shape,
        compiler_params=pltpu.CompilerParams(
            dimension_semantics=("arbitrary",),
            vmem_limit_bytes=_vmem_limit(blk, 2 * _nbytes((k, n), BF16), temp_bytes=3 * _nbytes((bm, n), F32))),
        name="out_proj",
    )(xa, xb_src, w, w, res, norm_w.reshape(1, n))


_CONV_TAIL = 16
_CONV_COLS = 512


def _conv_shift_matrix():
    L = SSD_CHUNK
    r = jnp.arange((SSD_CONV - 1) * L)
    src = _CONV_TAIL + (r % L) - (SSD_CONV - 1 - r // L)
    return (src[:, None] == jnp.arange(_CONV_TAIL + L)[None, :]).astype(BF16)


def _ssd_kernel(z_ref, xs_ref, bc_ref, dtr_ref, cw_ref, cb_ref, dtb_ref, alog_ref, dskip_ref,
                nw_ref, exp_ref, shift_ref, o_ref, tail_ref, xc_ref, state_ref):
    L = SSD_CHUNK
    c = pl.program_id(1)

    @pl.when(c == 0)
    def _():
        state_ref[...] = jnp.zeros_like(state_ref)
        tail_ref[...] = jnp.zeros_like(tail_ref)

    for j in range(xc_ref.shape[1] // _CONV_COLS):
        cols = slice(j * _CONV_COLS, (j + 1) * _CONV_COLS)
        src_ref, c_src = (xs_ref, j * _CONV_COLS) if j * _CONV_COLS < SSD_WIDTH else (bc_ref, j * _CONV_COLS - SSD_WIDTH)
        cur = src_ref[:, c_src:c_src + _CONV_COLS]
        shifted = _dot(shift_ref[...], jnp.concatenate([tail_ref[:, cols], cur], axis=0))
        acc = cb_ref[:, cols] + cw_ref[SSD_CONV - 1:SSD_CONV, cols] * cur.astype(F32)
        for k in range(SSD_CONV - 1):
            acc = acc + cw_ref[k:k + 1, cols] * shifted[k * L:(k + 1) * L, :]
        xc_ref[:, cols] = _silu(acc)
        tail_ref[:, cols] = cur[L - _CONV_TAIL:L, :]

    x_dt = dtr_ref[...] + dtb_ref[...]
    dt = jnp.maximum(x_dt, 0.0) + jnp.log1p(jnp.exp(-jnp.abs(x_dt)))
    da = dt * (-jnp.exp(alog_ref[...]))
    row = lax.broadcasted_iota(jnp.int32, (L, L), 0)
    col = lax.broadcasted_iota(jnp.int32, (L, L), 1)
    causal = col <= row
    tri = jnp.where(causal, 1.0, 0.0).astype(BF16)
    cs = sum(_dot(tri, p) for p in _split_bf16(da, 3))
    cs_t = cs.T
    cs_last = cs[L - 1:L, :]
    exp_cs = jnp.exp(cs)
    w_state = jnp.exp(cs_last - cs) * dt
    chunk_decay = jnp.broadcast_to(jnp.exp(cs_last), (8, DT_PAD))

    def expand(a):
        return sum(_dot(p, exp_ref[...]) for p in _split_bf16(a, 2))

    dt_x = expand(dt)
    w_state_x = expand(w_state)
    chunk_decay_x = expand(chunk_decay)[0:1, :]

    lane = lax.broadcasted_iota(jnp.int32, (2 * L, SSD_GROUP_WIDTH), 1)
    for g in range(SSD_GROUPS):
        gcols = slice(g * SSD_GROUP_WIDTH, (g + 1) * SSD_GROUP_WIDTH)
        xg = xc_ref[:, gcols]
        b0 = SSD_WIDTH + g * SSD_STATE
        c0 = SSD_WIDTH + SSD_BC_WIDTH + g * SSD_STATE
        bg = xc_ref[:, b0:b0 + SSD_STATE].astype(BF16)
        cg = xc_ref[:, c0:c0 + SSD_STATE]
        cbm = jnp.where(causal, _dot_nt(cg.astype(BF16), bg), 0.0)
        prev = state_ref[g]
        rhs_all = jnp.concatenate([(xg * dt_x[:, gcols]).astype(BF16), prev.astype(BF16)], axis=0)
        lhs_parts, rhs_parts = [], []
        for r in range(SSD_HEADS_PER_GROUP):
            h = g * SSD_HEADS_PER_GROUP + r
            seg = jnp.where(causal, cs[:, h:h + 1] - cs_t[h:h + 1, :], 0.0)
            lhs_parts.append((cbm * jnp.exp(seg)).astype(BF16))
            lhs_parts.append((cg * exp_cs[:, h:h + 1]).astype(BF16))
            in_head = (lane >= r * SSD_HEAD_DIM) & (lane < (r + 1) * SSD_HEAD_DIM)
            rhs_parts.append(jnp.where(in_head, rhs_all, jnp.zeros_like(rhs_all)))
        y = _dot(jnp.concatenate(lhs_parts, axis=1), jnp.concatenate(rhs_parts, axis=0))
        y = y + xg * dskip_ref[:, gcols]
        y = y * _silu(z_ref[:, gcols].astype(F32))
        ms = jnp.mean(y * y, axis=-1, keepdims=True)
        o_ref[:, gcols] = (y * lax.rsqrt(ms + EPS) * nw_ref[:, gcols]).astype(o_ref.dtype)
        xs_w = (xg * w_state_x[:, gcols]).astype(BF16)
        state_ref[g] = prev * chunk_decay_x[:, gcols] + _dot(bg.T, xs_w)


def _ssd(proj, dt_raw, conv_w, conv_b, dt_bias, a_log, d_skip, norm_w):
    bsz, s, _ = proj.shape
    L = SSD_CHUNK
    conv_dim = SSD_WIDTH + 2 * SSD_BC_WIDTH
    pad = DT_PAD - SSD_HEADS
    head_of_col = jnp.arange(SSD_WIDTH) // SSD_HEAD_DIM
    expand_mat = (jnp.arange(DT_PAD)[:, None] == head_of_col[None, :]).astype(BF16)
    blk = (3 * _nbytes((L, SSD_WIDTH), BF16) + _nbytes((L, DT_PAD), F32) + _nbytes((L, SSD_WIDTH), BF16)
           + _nbytes((SSD_CONV + 1, conv_dim), F32) + _nbytes((DT_PAD, SSD_WIDTH), BF16))
    shift_mat = _conv_shift_matrix()
    blk += _nbytes(shift_mat.shape, BF16)
    scratch = (_nbytes((_CONV_TAIL, conv_dim), BF16) + _nbytes((L, conv_dim), F32)
               + _nbytes((SSD_GROUPS, SSD_STATE, SSD_GROUP_WIDTH), F32))
    row = lambda b, c: (b, c, 0)
    const = lambda b, c: (0, 0)
    return pl.pallas_call(
        _ssd_kernel,
        grid=(bsz, s // L),
        in_specs=[pl.BlockSpec((None, L, SSD_WIDTH), lambda b, c: (b, c, 0)),
                  pl.BlockSpec((None, L, SSD_WIDTH), lambda b, c: (b, c, 1)),
                  pl.BlockSpec((None, L, SSD_WIDTH), lambda b, c: (b, c, 2)),
                  pl.BlockSpec((None, L, DT_PAD), row),
                  pl.BlockSpec((SSD_CONV, conv_dim), const),
                  pl.BlockSpec((1, conv_dim), const),
                  pl.BlockSpec((1, DT_PAD), const),
                  pl.BlockSpec((1, DT_PAD), const),
                  pl.BlockSpec((1, SSD_WIDTH), const),
                  pl.BlockSpec((1, SSD_WIDTH), const),
                  pl.BlockSpec((DT_PAD, SSD_WIDTH), const),
                  pl.BlockSpec(shift_mat.shape, const)],
        out_specs=pl.BlockSpec((None, L, SSD_WIDTH), row),
        out_shape=jax.ShapeDtypeStruct((bsz, s, SSD_WIDTH), BF16),
        scratch_shapes=[pltpu.VMEM((_CONV_TAIL, conv_dim), BF16),
                        pltpu.VMEM((L, conv_dim), F32),
                        pltpu.VMEM((SSD_GROUPS, SSD_STATE, SSD_GROUP_WIDTH), F32)],
        compiler_params=pltpu.CompilerParams(
            dimension_semantics=("arbitrary", "arbitrary"),
            vmem_limit_bytes=_vmem_limit(blk, scratch, temp_bytes=16 << 20)),
        name="ssd",
    )(proj, proj, proj, dt_raw, conv_w, conv_b.reshape(1, conv_dim),
      jnp.pad(dt_bias, (0, pad)).reshape(1, DT_PAD), jnp.pad(a_log, (0, pad)).reshape(1, DT_PAD),
      jnp.repeat(d_skip, SSD_HEAD_DIM).reshape(1, SSD_WIDTH), norm_w.reshape(1, SSD_WIDTH), expand_mat,
      shift_mat)


_SGU_CHUNKS_PER_STEP = 2


def _sgu_kernel(zb_ref, u_ref, v_ref, lnw_ref, lnb_ref, ws_ref, bias_ref, o_ref):
    T = SGU_CHUNK
    v = _gelu_tanh(v_ref[...].astype(F32))
    mu = jnp.mean(v, axis=-1, keepdims=True)
    vc = v - mu
    var = jnp.mean(vc * vc, axis=-1, keepdims=True)
    vn = (vc * lax.rsqrt(var + EPS) * lnw_ref[...] + lnb_ref[...]).astype(BF16)
    row = lax.broadcasted_iota(jnp.int32, (T, T), 0)
    col = lax.broadcasted_iota(jnp.int32, (T, T), 1)
    for g in range(SGU_GROUPS):
        gcols = slice(g * SGU_GROUP_DIM, (g + 1) * SGU_GROUP_DIM)
        ws = jnp.where(col <= row, ws_ref[g], 0.0).astype(BF16)
        rhs = jnp.concatenate([vn[k * T:(k + 1) * T, gcols] for k in range(_SGU_CHUNKS_PER_STEP)], axis=1)
        mixed = _dot(ws, rhs) + bias_ref[:, g:g + 1]
        for k in range(_SGU_CHUNKS_PER_STEP):
            rows = slice(k * T, (k + 1) * T)
            u = _gelu_tanh(u_ref[rows, gcols].astype(F32))
            gate = _silu(zb_ref[rows, gcols].astype(F32))
            o_ref[rows, gcols] = (u * mixed[:, k * SGU_GROUP_DIM:(k + 1) * SGU_GROUP_DIM] * gate).astype(o_ref.dtype)


def _sgu(proj2d, col0, ln_w, ln_b, ws, bias):
    m = proj2d.shape[0]
    bm = _SGU_CHUNKS_PER_STEP * SGU_CHUNK
    blk = 4 * _nbytes((bm, SGU_WIDTH), BF16) + _nbytes((SGU_GROUPS, SGU_CHUNK, SGU_CHUNK), F32)
    const2 = lambda i: (0, 0)
    return pl.pallas_call(
        _sgu_kernel,
        grid=(m // bm,),
        in_specs=[pl.BlockSpec((bm, SGU_WIDTH), lambda i: (i, col0)),
                  pl.BlockSpec((bm, SGU_WIDTH), lambda i: (i, col0 + 1)),
                  pl.BlockSpec((bm, SGU_WIDTH), lambda i: (i, col0 + 2)),
                  pl.BlockSpec((1, SGU_WIDTH), const2),
                  pl.BlockSpec((1, SGU_WIDTH), const2),
                  pl.BlockSpec((SGU_GROUPS, SGU_CHUNK, SGU_CHUNK), lambda i: (0, 0, 0)),
                  pl.BlockSpec((SGU_CHUNK, SGU_GROUPS), const2)],
        out_specs=pl.BlockSpec((bm, SGU_WIDTH), lambda i: (i, 0)),
        out_shape=jax.ShapeDtypeStruct((m, SGU_WIDTH), BF16),
        compiler_params=pltpu.CompilerParams(
            dimension_semantics=("arbitrary",),
            vmem_limit_bytes=_vmem_limit(blk, temp_bytes=6 * _nbytes((bm, SGU_WIDTH), F32))),
        name="sgu",
    )(proj2d, proj2d, proj2d, ln_w.reshape(1, SGU_WIDTH), ln_b.reshape(1, SGU_WIDTH), ws, bias.T)


_ATTN_BQ = 256
_ATTN_BK = 256
_MASKED = -1e30


ATTN_Q_SCALE = DIFF_HEAD_DIM ** -0.5 * math.log2(math.e)


def _softmax_block(st):
    m = jnp.max(st, axis=0, keepdims=True)
    e = jnp.exp2(st - m)
    return e, m, jnp.sum(e, axis=0, keepdims=True)


def _attn_kernel(q_ref, k_ref, v_ref, g_ref, lq1_ref, lk1_ref, lq2_ref, lk2_ref, sw_ref, o_ref, *,
                 lambda_init):
    s_len = q_ref.shape[0]
    d = DIFF_HEAD_DIM
    bq, bk = _ATTN_BQ, _ATTN_BK
    lam = (jnp.exp(jnp.sum(lq1_ref[...] * lk1_ref[...], axis=-1, keepdims=True))
           - jnp.exp(jnp.sum(lq2_ref[...] * lk2_ref[...], axis=-1, keepdims=True)) + lambda_init)
    key_idx = lax.broadcasted_iota(jnp.int32, (bk, bq), 0)
    query_idx = lax.broadcasted_iota(jnp.int32, (bk, bq), 1)
    v_t = v_ref[...].astype(F32).T.astype(BF16)
    for i in range(s_len // bq):
        lo, hi = i * bq, (i + 1) * bq
        q = q_ref[lo:hi, :]
        n_blocks = hi // bk
        weights = []
        for j in range(2):
            qj = q[:, j * d:(j + 1) * d]
            blocks = []
            for t in range(n_blocks):
                blk = _dot_nt(k_ref[t * bk:(t + 1) * bk, j * d:(j + 1) * d], qj)
                if (t + 1) * bk - 1 > lo:
                    blk = jnp.where(key_idx + (t * bk - lo) <= query_idx, blk, _MASKED)
                blocks.append(_softmax_block(blk))
            m_all = functools.reduce(jnp.maximum, [b[1] for b in blocks])
            alphas = [jnp.exp2(b[1] - m_all) for b in blocks]
            total = sum(a * b[2] for a, b in zip(alphas, blocks))
            norm = (1.0 / total) if j == 0 else (lam / total)
            weights.append([(b[0], a * norm) for a, b in zip(alphas, blocks)])
        p_t = jnp.concatenate([(e1 * f1 - e2 * f2).astype(BF16)
                               for (e1, f1), (e2, f2) in zip(weights[0], weights[1])], axis=0)
        o = _dot(v_t[:, 0:hi], p_t).T
        ms = jnp.mean(o * o, axis=-1, keepdims=True)
        o = o * lax.rsqrt(ms + EPS) * sw_ref[...] * (1.0 - lambda_init)
        o_ref[lo:hi, :] = (o * _silu(g_ref[lo:hi, :].astype(F32))).astype(o_ref.dtype)


def _diff_attention(proj, lq1, lk1, lq2, lk2, subln_w, lambda_init):
    bsz, s, _ = proj.shape
    blk = 5 * _nbytes((s, DIFF_V_DIM), BF16)
    vec = lambda a: a.reshape(1, -1)
    const = lambda b, h: (0, 0)
    return pl.pallas_call(
        functools.partial(_attn_kernel, lambda_init=lambda_init),
        grid=(bsz, DIFF_HEADS),
        in_specs=[pl.BlockSpec((None, s, DIFF_V_DIM), lambda b, h: (b, 0, h)),
                  pl.BlockSpec((None, s, DIFF_V_DIM), lambda b, h: (b, 0, DIFF_HEADS + h)),
                  pl.BlockSpec((None, s, DIFF_V_DIM), lambda b, h: (b, 0, 2 * DIFF_HEADS + h)),
                  pl.BlockSpec((None, s, DIFF_V_DIM), lambda b, h: (b, 0, 3 * DIFF_HEADS + h)),
                  pl.BlockSpec((1, DIFF_HEAD_DIM), const),
                  pl.BlockSpec((1, DIFF_HEAD_DIM), const),
                  pl.BlockSpec((1, DIFF_HEAD_DIM), const),
                  pl.BlockSpec((1, DIFF_HEAD_DIM), const),
                  pl.BlockSpec((1, DIFF_V_DIM), const)],
        out_specs=pl.BlockSpec((None, s, DIFF_V_DIM), lambda b, h: (b, 0, h)),
        out_shape=jax.ShapeDtypeStruct((bsz, s, DIFF_WIDTH), BF16),
        compiler_params=pltpu.CompilerParams(
            dimension_semantics=("arbitrary", "arbitrary"),
            vmem_limit_bytes=_vmem_limit(blk, temp_bytes=8 * _nbytes((_ATTN_BQ, s), F32))),
        name="diff_attn",
    )(proj, proj, proj, proj, vec(lq1), vec(lk1), vec(lq2), vec(lk2), vec(subln_w))


def kernel(x, norm_w, even_w_in, even_conv_w, even_conv_b, even_dt_bias, even_a_log, even_d_skip,
           even_ssd_norm_w, even_sgu_ln_w, even_sgu_ln_b, even_sgu_ws, even_sgu_b, even_w_out,
           odd_w_in, odd_lam_q1, odd_lam_k1, odd_lam_q2, odd_lam_k2, odd_subln_w, odd_w_out,
           final_norm_w):
    bsz, s, d = x.shape
    m = bsz * s
    x2 = x.reshape(m, d)

    w_in_t = jnp.swapaxes(even_w_in, 1, 2)
    hn, dt_raw = _rmsnorm_dt(x2, norm_w[0], w_in_t)
    n_main = even_w_in.shape[2] - SSD_HEADS
    proj = _in_proj(hn, w_in_t, n_main, transposed=True, skip_col=SSD_DT_COL,
                    skip=SSD_HEADS)
    y_a = _ssd(proj.reshape(bsz, s, -1), dt_raw.reshape(bsz, s, DT_PAD), even_conv_w[0], even_conv_b[0],
               even_dt_bias[0], even_a_log[0], even_d_skip[0], even_ssd_norm_w[0])
    y_b = _sgu(proj, 3, even_sgu_ln_w[0], even_sgu_ln_b[0], even_sgu_ws[0], even_sgu_b[0])
    h1, hn1 = _out_proj(y_a.reshape(m, SSD_WIDTH), y_b, 0, even_w_out[0].astype(BF16), x2, norm_w[1],
                        BF16, keep_residual=True)

    lambda_init = 0.8 - 0.6 * math.exp(-0.3 * 1)
    n_in1 = odd_w_in.shape[2]
    q_scale = jnp.where(jnp.arange(n_in1) < DIFF_WIDTH, ATTN_Q_SCALE, 1.0).astype(F32)
    proj1 = _in_proj(hn1, odd_w_in, n_in1, col_scale=q_scale)
    o = _diff_attention(proj1.reshape(bsz, s, -1), odd_lam_q1[0], odd_lam_k1[0], odd_lam_q2[0],
                        odd_lam_k2[0], odd_subln_w[0], lambda_init)
    o2 = o.reshape(m, DIFF_WIDTH)
    (out,) = _out_proj(o2, o2, 1, odd_w_out[0].astype(BF16), h1, final_norm_w, F32, keep_residual=False)
    return out.reshape(bsz, s, d)
```

```python
import functools
import math

import jax
import jax.numpy as jnp
from jax import lax
from jax.experimental import pallas as pl
from jax.experimental.pallas import tpu as pltpu

F32 = jnp.float32
BF16 = jnp.bfloat16

D_MODEL = 2048
SSD_WIDTH = 2048
SSD_HEAD_DIM = 64
SSD_HEADS = 32
SSD_GROUPS = 8
SSD_HEADS_PER_GROUP = SSD_HEADS // SSD_GROUPS
SSD_GROUP_WIDTH = SSD_WIDTH // SSD_GROUPS
SSD_STATE = 128
SSD_CONV = 4
SSD_CHUNK = 128
SSD_BC_WIDTH = SSD_GROUPS * SSD_STATE
SSD_DT_COL = 2 * SSD_WIDTH + 2 * SSD_BC_WIDTH
SGU_WIDTH = 2048
SGU_CHUNK = 128
SGU_GROUPS = 16
SGU_GROUP_DIM = 128
DIFF_HEADS = 16
DIFF_HEAD_DIM = 128
DIFF_V_DIM = 256
DIFF_WIDTH = DIFF_HEADS * DIFF_V_DIM
EPS = 1e-6

V7X_LANES = 128
V7X_VMEM_BYTES = 64 * 1024 * 1024
DT_PAD = V7X_LANES


def _vmem_limit(block_bytes, scratch_bytes=0, temp_bytes=0):
    est = 2 * block_bytes + scratch_bytes + temp_bytes + (4 << 20)
    return int(min(est, V7X_VMEM_BYTES - (8 << 20)))


def _nbytes(shape, dtype):
    return math.prod(shape) * jnp.dtype(dtype).itemsize


def _silu(x):
    h = 0.5 * x
    return h + h * jnp.tanh(h)


def _gelu_tanh(x):
    c = math.sqrt(2.0 / math.pi)
    h = 0.5 * x
    return h + h * jnp.tanh(x * (c + (c * 0.044715) * (x * x)))


def _split_bf16(x, n):
    parts = []
    r = x
    for i in range(n):
        p = r.astype(BF16)
        parts.append(p)
        if i + 1 < n:
            r = r - p.astype(F32)
    return parts


def _dot(a, b):
    return jnp.dot(a, b, preferred_element_type=F32)


def _dot_nt(a, b):
    return lax.dot_general(a, b, (((1,), (1,)), ((), ())), preferred_element_type=F32)


def _rms_dt_kernel(x_ref, w_ref, wdt_ref, o_ref, dt_ref):
    x = x_ref[...]
    ms = jnp.mean(x * x, axis=-1, keepdims=True)
    y = x * lax.rsqrt(ms + EPS) * w_ref[...]
    y_hi, y_lo = _split_bf16(y, 2)
    o_ref[...] = y_hi
    row = lax.broadcasted_iota(jnp.int32, wdt_ref.shape, 0)
    w_hi, w_lo = _split_bf16(jnp.where(row < SSD_HEADS, wdt_ref[...], 0.0), 2)
    dt_ref[...] = _dot_nt(y_hi, w_hi) + (_dot_nt(y_lo, w_hi) + _dot_nt(y_hi, w_lo))


def _rmsnorm_dt(x, w, w_in_t, bm=512):
    m, d = x.shape
    blk = _nbytes((bm, d), F32) + _nbytes((bm, d), BF16) + _nbytes((DT_PAD, d), F32)
    return pl.pallas_call(
        _rms_dt_kernel,
        grid=(m // bm,),
        in_specs=[pl.BlockSpec((bm, d), lambda i: (i, 0)),
                  pl.BlockSpec((1, d), lambda i: (0, 0)),
                  pl.BlockSpec((None, DT_PAD, d), lambda i: (0, SSD_DT_COL // DT_PAD, 0))],
        out_specs=[pl.BlockSpec((bm, d), lambda i: (i, 0)),
                   pl.BlockSpec((bm, DT_PAD), lambda i: (i, 0))],
        out_shape=[jax.ShapeDtypeStruct((m, d), BF16),
                   jax.ShapeDtypeStruct((m, DT_PAD), F32)],
        compiler_params=pltpu.CompilerParams(
            dimension_semantics=("arbitrary",),
            vmem_limit_bytes=_vmem_limit(blk, temp_bytes=4 * _nbytes((bm, d), F32))),
        name="rmsnorm_dt",
    )(x, w.reshape(1, d), w_in_t)


_CAST_ROWS = 256


def _in_proj_kernel(x_ref, w_ref, *rest, transposed, scaled):
    rest = list(rest)
    scale_ref = rest.pop(0) if scaled else None
    o_ref, wbf_ref = rest

    @pl.when(pl.program_id(1) == 0)
    def _():
        for r in range(0, wbf_ref.shape[0], _CAST_ROWS):
            wbf_ref[r:r + _CAST_ROWS, :] = w_ref[r:r + _CAST_ROWS, :].astype(BF16)

    acc = (_dot_nt if transposed else _dot)(x_ref[...], wbf_ref[...])
    if scaled:
        acc = acc * scale_ref[...]
    o_ref[...] = acc.astype(o_ref.dtype)


def _in_proj(x, w3, n_out, transposed=False, skip_col=None, skip=0, col_scale=None, bm=2048, bn=1024):
    m, k = x.shape
    if transposed:
        assert skip % 8 == 0 and (skip == 0 or skip_col % bn == 0)
        start = lambda j: 8 * (j * (bn // 8) + (jnp.where(j * bn >= skip_col, skip // 8, 0) if skip else 0))
        w_spec = pl.BlockSpec((None, pl.Element(bn), pl.Element(k)), lambda j, i: (0, start(j), 0))
        w_block = (bn, k)
    else:
        assert not skip
        w_spec = pl.BlockSpec((None, k, bn), lambda j, i: (0, 0, j))
        w_block = (k, bn)
    in_specs = [pl.BlockSpec((bm, k), lambda j, i: (i, 0)), w_spec]
    args = [x, w3]
    blk = _nbytes((bm, k), BF16) + _nbytes(w_block, F32) + _nbytes((bm, bn), BF16)
    if col_scale is not None:
        in_specs.append(pl.BlockSpec((1, bn), lambda j, i: (0, j)))
        args.append(col_scale.reshape(1, n_out))
    return pl.pallas_call(
        functools.partial(_in_proj_kernel, transposed=transposed, scaled=col_scale is not None),
        grid=(n_out // bn, m // bm),
        in_specs=in_specs,
        out_specs=pl.BlockSpec((bm, bn), lambda j, i: (i, j)),
        out_shape=jax.ShapeDtypeStruct((m, n_out), BF16),
        scratch_shapes=[pltpu.VMEM(w_block, BF16)],
        compiler_params=pltpu.CompilerParams(
            dimension_semantics=("arbitrary", "arbitrary"),
            vmem_limit_bytes=_vmem_limit(blk, _nbytes(w_block, BF16), temp_bytes=_nbytes((bm, bn), F32))),
        name="in_proj",
    )(*args)


def _out_proj_kernel(xa_ref, xb_ref, wa_ref, wb_ref, r_ref, nw_ref, *out_refs):
    h = r_ref[...] + (_dot(xa_ref[...], wa_ref[...]) + _dot(xb_ref[...], wb_ref[...]))
    if len(out_refs) == 2:
        out_refs[0][...] = h
    hn_ref = out_refs[-1]
    ms = jnp.mean(h * h, axis=-1, keepdims=True)
    hn_ref[...] = (h * lax.rsqrt(ms + EPS) * nw_ref[...]).astype(hn_ref.dtype)


def _out_proj(xa, xb_src, xb_col, w, res, norm_w, norm_dtype, keep_residual, bm=512):
    m = xa.shape[0]
    k = w.shape[0] // 2
    n = w.shape[1]
    blk = 2 * _nbytes((bm, k), BF16) + 2 * _nbytes((bm, n), F32) + _nbytes((bm, n), norm_dtype)
    row = lambda i: (i, 0)
    once = pl.Buffered(1)
    out_specs = [pl.BlockSpec((bm, n), row)]
    out_shape = [jax.ShapeDtypeStruct((m, n), norm_dtype)]
    if keep_residual:
        out_specs.insert(0, pl.BlockSpec((bm, n), row))
        out_shape.insert(0, jax.ShapeDtypeStruct((m, n), F32))
    return pl.pallas_call(
        _out_proj_kernel,
        grid=(m // bm,),
        in_specs=[pl.BlockSpec((bm, k), row),
                  pl.BlockSpec((bm, k), lambda i: (i, xb_col)),
                  pl.BlockSpec((k, n), lambda i: (0, 0), pipeline_mode=once),
                  pl.BlockSpec((k, n), lambda i: (1, 0), pipeline_mode=once),
                  pl.BlockSpec((bm, n), row),
                  pl.BlockSpec((1, n), lambda i: (0, 0))],
        out_specs=out_specs,
        out_shape=out_shape,
        compiler_params=pltpu.CompilerParams(
            dimension_semantics=("arbitrary",),
            vmem_limit_bytes=_vmem_limit(blk, 2 * _nbytes((k, n), BF16), temp_bytes=3 * _nbytes((bm, n), F32))),
        name="out_proj",
    )(xa, xb_src, w, w, res, norm_w.reshape(1, n))


_CONV_TAIL = 16
_CONV_COLS = 512


def _conv_shift_matrix():
    L = SSD_CHUNK
    r = jnp.arange((SSD_CONV - 1) * L)
    src = _CONV_TAIL + (r % L) - (SSD_CONV - 1 - r // L)
    return (src[:, None] == jnp.arange(_CONV_TAIL + L)[None, :]).astype(BF16)


def _ssd_kernel(z_ref, xs_ref, bc_ref, dtr_ref, cw_ref, cb_ref, dtb_ref, alog_ref, dskip_ref,
                nw_ref, exp_ref, shift_ref, o_ref, tail_ref, xc_ref, state_ref):
    L = SSD_CHUNK
    c = pl.program_id(1)

    @pl.when(c == 0)
    def _():
        state_ref[...] = jnp.zeros_like(state_ref)
        tail_ref[...] = jnp.zeros_like(tail_ref)

    for j in range(xc_ref.shape[1] // _CONV_COLS):
        cols = slice(j * _CONV_COLS, (j + 1) * _CONV_COLS)
        src_ref, c_src = (xs_ref, j * _CONV_COLS) if j * _CONV_COLS < SSD_WIDTH else (bc_ref, j * _CONV_COLS - SSD_WIDTH)
        cur = src_ref[:, c_src:c_src + _CONV_COLS]
        shifted = _dot(shift_ref[...], jnp.concatenate([tail_ref[:, cols], cur], axis=0))
        acc = cb_ref[:, cols] + cw_ref[SSD_CONV - 1:SSD_CONV, cols] * cur.astype(F32)
        for k in range(SSD_CONV - 1):
            acc = acc + cw_ref[k:k + 1, cols] * shifted[k * L:(k + 1) * L, :]
        xc_ref[:, cols] = _silu(acc)
        tail_ref[:, cols] = cur[L - _CONV_TAIL:L, :]

    x_dt = dtr_ref[...] + dtb_ref[...]
    dt = jnp.maximum(x_dt, 0.0) + jnp.log1p(jnp.exp(-jnp.abs(x_dt)))
    da = dt * (-jnp.exp(alog_ref[...]))
    row = lax.broadcasted_iota(jnp.int32, (L, L), 0)
    col = lax.broadcasted_iota(jnp.int32, (L, L), 1)
    causal = col <= row
    tri = jnp.where(causal, 1.0, 0.0).astype(BF16)
    cs = sum(_dot(tri, p) for p in _split_bf16(da, 3))
    cs_t = cs.T
    cs_last = cs[L - 1:L, :]
    exp_cs = jnp.exp(cs)
    w_state = jnp.exp(cs_last - cs) * dt
    chunk_decay = jnp.broadcast_to(jnp.exp(cs_last), (8, DT_PAD))

    def expand(a):
        return sum(_dot(p, exp_ref[...]) for p in _split_bf16(a, 2))

    dt_x = expand(dt)
    w_state_x = expand(w_state)
    chunk_decay_x = expand(chunk_decay)[0:1, :]

    lane = lax.broadcasted_iota(jnp.int32, (2 * L, SSD_GROUP_WIDTH), 1)
    for g in range(SSD_GROUPS):
        gcols = slice(g * SSD_GROUP_WIDTH, (g + 1) * SSD_GROUP_WIDTH)
        xg = xc_ref[:, gcols]
        b0 = SSD_WIDTH + g * SSD_STATE
        c0 = SSD_WIDTH + SSD_BC_WIDTH + g * SSD_STATE
        bg = xc_ref[:, b0:b0 + SSD_STATE].astype(BF16)
        cg = xc_ref[:, c0:c0 + SSD_STATE]
        cbm = jnp.where(causal, _dot_nt(cg.astype(BF16), bg), 0.0)
        prev = state_ref[g]
        rhs_all = jnp.concatenate([(xg * dt_x[:, gcols]).astype(BF16), prev.astype(BF16)], axis=0)
        lhs_parts, rhs_parts = [], []
        for r in range(SSD_HEADS_PER_GROUP):
            h = g * SSD_HEADS_PER_GROUP + r
            seg = jnp.where(causal, cs[:, h:h + 1] - cs_t[h:h + 1, :], 0.0)
            lhs_parts.append((cbm * jnp.exp(seg)).astype(BF16))
            lhs_parts.append((cg * exp_cs[:, h:h + 1]).astype(BF16))
            in_head = (lane >= r * SSD_HEAD_DIM) & (lane < (r + 1) * SSD_HEAD_DIM)
            rhs_parts.append(jnp.where(in_head, rhs_all, jnp.zeros_like(rhs_all)))
        y = _dot(jnp.concatenate(lhs_parts, axis=1), jnp.concatenate(rhs_parts, axis=0))
        y = y + xg * dskip_ref[:, gcols]
        y = y * _silu(z_ref[:, gcols].astype(F32))
        ms = jnp.mean(y * y, axis=-1, keepdims=True)
        o_ref[:, gcols] = (y * lax.rsqrt(ms + EPS) * nw_ref[:, gcols]).astype(o_ref.dtype)
        xs_w = (xg * w_state_x[:, gcols]).astype(BF16)
        state_ref[g] = prev * chunk_decay_x[:, gcols] + _dot(bg.T, xs_w)


def _ssd(proj, dt_raw, conv_w, conv_b, dt_bias, a_log, d_skip, norm_w):
    bsz, s, _ = proj.shape
    L = SSD_CHUNK
    conv_dim = SSD_WIDTH + 2 * SSD_BC_WIDTH
    pad = DT_PAD - SSD_HEADS
    head_of_col = jnp.arange(SSD_WIDTH) // SSD_HEAD_DIM
    expand_mat = (jnp.arange(DT_PAD)[:, None] == head_of_col[None, :]).astype(BF16)
    blk = (3 * _nbytes((L, SSD_WIDTH), BF16) + _nbytes((L, DT_PAD), F32) + _nbytes((L, SSD_WIDTH), BF16)
           + _nbytes((SSD_CONV + 1, conv_dim), F32) + _nbytes((DT_PAD, SSD_WIDTH), BF16))
    shift_mat = _conv_shift_matrix()
    blk += _nbytes(shift_mat.shape, BF16)
    scratch = (_nbytes((_CONV_TAIL, conv_dim), BF16) + _nbytes((L, conv_dim), F32)
               + _nbytes((SSD_GROUPS, SSD_STATE, SSD_GROUP_WIDTH), F32))
    row = lambda b, c: (b, c, 0)
    const = lambda b, c: (0, 0)
    return pl.pallas_call(
        _ssd_kernel,
        grid=(bsz, s // L),
        in_specs=[pl.BlockSpec((None, L, SSD_WIDTH), lambda b, c: (b, c, 0)),
                  pl.BlockSpec((None, L, SSD_WIDTH), lambda b, c: (b, c, 1)),
                  pl.BlockSpec((None, L, SSD_WIDTH), lambda b, c: (b, c, 2)),
                  pl.BlockSpec((None, L, DT_PAD), row),
                  pl.BlockSpec((SSD_CONV, conv_dim), const),
                  pl.BlockSpec((1, conv_dim), const),
                  pl.BlockSpec((1, DT_PAD), const),
                  pl.BlockSpec((1, DT_PAD), const),
                  pl.BlockSpec((1, SSD_WIDTH), const),
                  pl.BlockSpec((1, SSD_WIDTH), const),
                  pl.BlockSpec((DT_PAD, SSD_WIDTH), const),
                  pl.BlockSpec(shift_mat.shape, const)],
        out_specs=pl.BlockSpec((None, L, SSD_WIDTH), row),
        out_shape=jax.ShapeDtypeStruct((bsz, s, SSD_WIDTH), BF16),
        scratch_shapes=[pltpu.VMEM((_CONV_TAIL, conv_dim), BF16),
                        pltpu.VMEM((L, conv_dim), F32),
                        pltpu.VMEM((SSD_GROUPS, SSD_STATE, SSD_GROUP_WIDTH), F32)],
        compiler_params=pltpu.CompilerParams(
            dimension_semantics=("arbitrary", "arbitrary"),
            vmem_limit_bytes=_vmem_limit(blk, scratch, temp_bytes=16 << 20)),
        name="ssd",
    )(proj, proj, proj, dt_raw, conv_w, conv_b.reshape(1, conv_dim),
      jnp.pad(dt_bias, (0, pad)).reshape(1, DT_PAD), jnp.pad(a_log, (0, pad)).reshape(1, DT_PAD),
      jnp.repeat(d_skip, SSD_HEAD_DIM).reshape(1, SSD_WIDTH), norm_w.reshape(1, SSD_WIDTH), expand_mat,
      shift_mat)


_SGU_CHUNKS_PER_STEP = 2


def _sgu_kernel(zb_ref, u_ref, v_ref, lnw_ref, lnb_ref, ws_ref, bias_ref, o_ref):
    T = SGU_CHUNK
    v = _gelu_tanh(v_ref[...].astype(F32))
    mu = jnp.mean(v, axis=-1, keepdims=True)
    vc = v - mu
    var = jnp.mean(vc * vc, axis=-1, keepdims=True)
    vn = (vc * lax.rsqrt(var + EPS) * lnw_ref[...] + lnb_ref[...]).astype(BF16)
    row = lax.broadcasted_iota(jnp.int32, (T, T), 0)
    col = lax.broadcasted_iota(jnp.int32, (T, T), 1)
    for g in range(SGU_GROUPS):
        gcols = slice(g * SGU_GROUP_DIM, (g + 1) * SGU_GROUP_DIM)
        ws = jnp.where(col <= row, ws_ref[g], 0.0).astype(BF16)
        rhs = jnp.concatenate([vn[k * T:(k + 1) * T, gcols] for k in range(_SGU_CHUNKS_PER_STEP)], axis=1)
        mixed = _dot(ws, rhs) + bias_ref[:, g:g + 1]
        for k in range(_SGU_CHUNKS_PER_STEP):
            rows = slice(k * T, (k + 1) * T)
            u = _gelu_tanh(u_ref[rows, gcols].astype(F32))
            gate = _silu(zb_ref[rows, gcols].astype(F32))
            o_ref[rows, gcols] = (u * mixed[:, k * SGU_GROUP_DIM:(k + 1) * SGU_GROUP_DIM] * gate).astype(o_ref.dtype)


def _sgu(proj2d, col0, ln_w, ln_b, ws, bias):
    m = proj2d.shape[0]
    bm = _SGU_CHUNKS_PER_STEP * SGU_CHUNK
    blk = 4 * _nbytes((bm, SGU_WIDTH), BF16) + _nbytes((SGU_GROUPS, SGU_CHUNK, SGU_CHUNK), F32)
    const2 = lambda i: (0, 0)
    return pl.pallas_call(
        _sgu_kernel,
        grid=(m // bm,),
        in_specs=[pl.BlockSpec((bm, SGU_WIDTH), lambda i: (i, col0)),
                  pl.BlockSpec((bm, SGU_WIDTH), lambda i: (i, col0 + 1)),
                  pl.BlockSpec((bm, SGU_WIDTH), lambda i: (i, col0 + 2)),
                  pl.BlockSpec((1, SGU_WIDTH), const2),
                  pl.BlockSpec((1, SGU_WIDTH), const2),
                  pl.BlockSpec((SGU_GROUPS, SGU_CHUNK, SGU_CHUNK), lambda i: (0, 0, 0)),
                  pl.BlockSpec((SGU_CHUNK, SGU_GROUPS), const2)],
        out_specs=pl.BlockSpec((bm, SGU_WIDTH), lambda i: (i, 0)),
        out_shape=jax.ShapeDtypeStruct((m, SGU_WIDTH), BF16),
        compiler_params=pltpu.CompilerParams(
            dimension_semantics=("arbitrary",),
            vmem_limit_bytes=_vmem_limit(blk, temp_bytes=6 * _nbytes((bm, SGU_WIDTH), F32))),
        name="sgu",
    )(proj2d, proj2d, proj2d, ln_w.reshape(1, SGU_WIDTH), ln_b.reshape(1, SGU_WIDTH), ws, bias.T)


_ATTN_BQ = 256
_ATTN_BK = 256
_MASKED = -1e30


ATTN_Q_SCALE = DIFF_HEAD_DIM ** -0.5 * math.log2(math.e)


def _softmax_block(st):
    m = jnp.max(st, axis=0, keepdims=True)
    e = jnp.exp2(st - m)
    return e, m, jnp.sum(e, axis=0, keepdims=True)


def _attn_kernel(q_ref, k_ref, v_ref, g_ref, lq1_ref, lk1_ref, lq2_ref, lk2_ref, sw_ref, o_ref, *,
                 lambda_init):
    s_len = q_ref.shape[0]
    d = DIFF_HEAD_DIM
    bq, bk = _ATTN_BQ, _ATTN_BK
    lam = (jnp.exp(jnp.sum(lq1_ref[...] * lk1_ref[...], axis=-1, keepdims=True))
           - jnp.exp(jnp.sum(lq2_ref[...] * lk2_ref[...], axis=-1, keepdims=True)) + lambda_init)
    key_idx = lax.broadcasted_iota(jnp.int32, (bk, bq), 0)
    query_idx = lax.broadcasted_iota(jnp.int32, (bk, bq), 1)
    v_t = v_ref[...].astype(F32).T.astype(BF16)
    out_scale = sw_ref[...] * (1.0 - lambda_init)

    def combine(p_blocks, t, weights):
        (e1, f1), (e2, f2) = weights[0][t], weights[1][t]
        return (p_blocks or []) + [(e1 * f1 - e2 * f2).astype(BF16)]

    def finish(lo, hi, p_blocks):
        o = _dot(v_t[:, 0:hi], jnp.concatenate(p_blocks, axis=0)).T
        ms = jnp.mean(o * o, axis=-1, keepdims=True)
        o = o * lax.rsqrt(ms + EPS) * out_scale
        o_ref[lo:hi, :] = (o * _silu(g_ref[lo:hi, :].astype(F32))).astype(o_ref.dtype)

    n_tiles = s_len // bq
    pending = None
    for i in list(range(1, n_tiles)) + [0]:
        lo, hi = i * bq, (i + 1) * bq
        q = q_ref[lo:hi, :]
        n_blocks = hi // bk
        n_comb = 0 if pending is None else len(pending[2][0])
        blocks = ([], [])
        o_t = None
        for t in range(n_blocks):
            for j in range(2):
                blk = _dot_nt(k_ref[t * bk:(t + 1) * bk, j * d:(j + 1) * d], q[:, j * d:(j + 1) * d])
                if (t + 1) * bk - 1 > lo:
                    blk = jnp.where(key_idx + (t * bk - lo) <= query_idx, blk, _MASKED)
                blocks[j].append(_softmax_block(blk))
            for tc in range(t * n_comb // n_blocks, (t + 1) * n_comb // n_blocks):
                o_t = combine(o_t, tc, pending[2])
        if pending is not None:
            finish(pending[0], pending[1], o_t)
        weights = []
        for j in range(2):
            m_all = functools.reduce(jnp.maximum, [b[1] for b in blocks[j]])
            alphas = [jnp.exp2(b[1] - m_all) for b in blocks[j]]
            total = sum(a * b[2] for a, b in zip(alphas, blocks[j]))
            norm = (1.0 / total) if j == 0 else (lam / total)
            weights.append([(b[0], a * norm) for a, b in zip(alphas, blocks[j])])
        pending = (lo, hi, weights)
    o_t = None
    for t in range(len(pending[2][0])):
        o_t = combine(o_t, t, pending[2])
    finish(pending[0], pending[1], o_t)


def _diff_attention(proj, lq1, lk1, lq2, lk2, subln_w, lambda_init):
    bsz, s, _ = proj.shape
    blk = 5 * _nbytes((s, DIFF_V_DIM), BF16)
    vec = lambda a: a.reshape(1, -1)
    const = lambda b, h: (0, 0)
    return pl.pallas_call(
        functools.partial(_attn_kernel, lambda_init=lambda_init),
        grid=(bsz, DIFF_HEADS),
        in_specs=[pl.BlockSpec((None, s, DIFF_V_DIM), lambda b, h: (b, 0, h)),
                  pl.BlockSpec((None, s, DIFF_V_DIM), lambda b, h: (b, 0, DIFF_HEADS + h)),
                  pl.BlockSpec((None, s, DIFF_V_DIM), lambda b, h: (b, 0, 2 * DIFF_HEADS + h)),
                  pl.BlockSpec((None, s, DIFF_V_DIM), lambda b, h: (b, 0, 3 * DIFF_HEADS + h)),
                  pl.BlockSpec((1, DIFF_HEAD_DIM), const),
                  pl.BlockSpec((1, DIFF_HEAD_DIM), const),
                  pl.BlockSpec((1, DIFF_HEAD_DIM), const),
                  pl.BlockSpec((1, DIFF_HEAD_DIM), const),
                  pl.BlockSpec((1, DIFF_V_DIM), const)],
        out_specs=pl.BlockSpec((None, s, DIFF_V_DIM), lambda b, h: (b, 0, h)),
        out_shape=jax.ShapeDtypeStruct((bsz, s, DIFF_WIDTH), BF16),
        compiler_params=pltpu.CompilerParams(
            dimension_semantics=("arbitrary", "arbitrary"),
            vmem_limit_bytes=_vmem_limit(blk, temp_bytes=8 * _nbytes((_ATTN_BQ, s), F32))),
        name="diff_attn",
    )(proj, proj, proj, proj, vec(lq1), vec(lk1), vec(lq2), vec(lk2), vec(subln_w))


def kernel(x, norm_w, even_w_in, even_conv_w, even_conv_b, even_dt_bias, even_a_log, even_d_skip,
           even_ssd_norm_w, even_sgu_ln_w, even_sgu_ln_b, even_sgu_ws, even_sgu_b, even_w_out,
           odd_w_in, odd_lam_q1, odd_lam_k1, odd_lam_q2, odd_lam_k2, odd_subln_w, odd_w_out,
           final_norm_w):
    bsz, s, d = x.shape
    m = bsz * s
    x2 = x.reshape(m, d)

    w_in_t = jnp.swapaxes(even_w_in, 1, 2)
    hn, dt_raw = _rmsnorm_dt(x2, norm_w[0], w_in_t)
    n_main = even_w_in.shape[2] - SSD_HEADS
    proj = _in_proj(hn, w_in_t, n_main, transposed=True, skip_col=SSD_DT_COL,
                    skip=SSD_HEADS)
    y_a = _ssd(proj.reshape(bsz, s, -1), dt_raw.reshape(bsz, s, DT_PAD), even_conv_w[0], even_conv_b[0],
               even_dt_bias[0], even_a_log[0], even_d_skip[0], even_ssd_norm_w[0])
    y_b = _sgu(proj, 3, even_sgu_ln_w[0], even_sgu_ln_b[0], even_sgu_ws[0], even_sgu_b[0])
    h1, hn1 = _out_proj(y_a.reshape(m, SSD_WIDTH), y_b, 0, even_w_out[0].astype(BF16), x2, norm_w[1],
                        BF16, keep_residual=True)

    lambda_init = 0.8 - 0.6 * math.exp(-0.3 * 1)
    n_in1 = odd_w_in.shape[2]
    q_scale = jnp.where(jnp.arange(n_in1) < DIFF_WIDTH, ATTN_Q_SCALE, 1.0).astype(F32)
    proj1 = _in_proj(hn1, odd_w_in, n_in1, col_scale=q_scale)
    o = _diff_attention(proj1.reshape(bsz, s, -1), odd_lam_q1[0], odd_lam_k1[0], odd_lam_q2[0],
                        odd_lam_k2[0], odd_subln_w[0], lambda_init)
    o2 = o.reshape(m, DIFF_WIDTH)
    (out,) = _out_proj(o2, o2, 1, odd_w_out[0].astype(BF16), h1, final_norm_w, F32, keep_residual=False)
    return out.reshape(bsz, s, d)
```

```python
import functools
import math

import jax
import jax.numpy as jnp
from jax import lax
from jax.experimental import pallas as pl
from jax.experimental.pallas import tpu as pltpu

F32 = jnp.float32
BF16 = jnp.bfloat16

D_MODEL = 2048
SSD_WIDTH = 2048
SSD_HEAD_DIM = 64
SSD_HEADS = 32
SSD_GROUPS = 8
SSD_HEADS_PER_GROUP = SSD_HEADS // SSD_GROUPS
SSD_GROUP_WIDTH = SSD_WIDTH // SSD_GROUPS
SSD_STATE = 128
SSD_CONV = 4
SSD_CHUNK = 128
SSD_BC_WIDTH = SSD_GROUPS * SSD_STATE
SSD_DT_COL = 2 * SSD_WIDTH + 2 * SSD_BC_WIDTH
SGU_WIDTH = 2048
SGU_CHUNK = 128
SGU_GROUPS = 16
SGU_GROUP_DIM = 128
DIFF_HEADS = 16
DIFF_HEAD_DIM = 128
DIFF_V_DIM = 256
DIFF_WIDTH = DIFF_HEADS * DIFF_V_DIM
EPS = 1e-6

V7X_LANES = 128
V7X_VMEM_BYTES = 64 * 1024 * 1024
DT_PAD = V7X_LANES


def _vmem_limit(block_bytes, scratch_bytes=0, temp_bytes=0):
    est = 2 * block_bytes + scratch_bytes + temp_bytes + (4 << 20)
    return int(min(est, V7X_VMEM_BYTES - (8 << 20)))


def _nbytes(shape, dtype):
    return math.prod(shape) * jnp.dtype(dtype).itemsize


def _silu(x):
    h = 0.5 * x
    return h + h * jnp.tanh(h)


def _gelu_tanh(x):
    c = math.sqrt(2.0 / math.pi)
    h = 0.5 * x
    return h + h * jnp.tanh(x * (c + (c * 0.044715) * (x * x)))


def _split_bf16(x, n):
    parts = []
    r = x
    for i in range(n):
        p = r.astype(BF16)
        parts.append(p)
        if i + 1 < n:
            r = r - p.astype(F32)
    return parts


def _dot(a, b):
    return jnp.dot(a, b, preferred_element_type=F32)


def _dot_nt(a, b):
    return lax.dot_general(a, b, (((1,), (1,)), ((), ())), preferred_element_type=F32)


def _rms_dt_kernel(x_ref, w_ref, wdt_ref, o_ref, dt_ref):
    x = x_ref[...]
    ms = jnp.mean(x * x, axis=-1, keepdims=True)
    y = x * lax.rsqrt(ms + EPS) * w_ref[...]
    y_hi, y_lo = _split_bf16(y, 2)
    o_ref[...] = y_hi
    row = lax.broadcasted_iota(jnp.int32, wdt_ref.shape, 0)
    w_hi, w_lo = _split_bf16(jnp.where(row < SSD_HEADS, wdt_ref[...], 0.0), 2)
    dt_ref[...] = _dot_nt(y_hi, w_hi) + (_dot_nt(y_lo, w_hi) + _dot_nt(y_hi, w_lo))


def _rmsnorm_dt(x, w, w_in_t, bm=512):
    m, d = x.shape
    blk = _nbytes((bm, d), F32) + _nbytes((bm, d), BF16) + _nbytes((DT_PAD, d), F32)
    return pl.pallas_call(
        _rms_dt_kernel,
        grid=(m // bm,),
        in_specs=[pl.BlockSpec((bm, d), lambda i: (i, 0)),
                  pl.BlockSpec((1, d), lambda i: (0, 0)),
                  pl.BlockSpec((None, DT_PAD, d), lambda i: (0, SSD_DT_COL // DT_PAD, 0))],
        out_specs=[pl.BlockSpec((bm, d), lambda i: (i, 0)),
                   pl.BlockSpec((bm, DT_PAD), lambda i: (i, 0))],
        out_shape=[jax.ShapeDtypeStruct((m, d), BF16),
                   jax.ShapeDtypeStruct((m, DT_PAD), F32)],
        compiler_params=pltpu.CompilerParams(
            dimension_semantics=("arbitrary",),
            vmem_limit_bytes=_vmem_limit(blk, temp_bytes=4 * _nbytes((bm, d), F32))),
        name="rmsnorm_dt",
    )(x, w.reshape(1, d), w_in_t)


_CAST_ROWS = 256


def _in_proj_kernel(x_ref, w_ref, *rest, transposed, scaled):
    rest = list(rest)
    scale_ref = rest.pop(0) if scaled else None
    o_ref, wbf_ref = rest

    @pl.when(pl.program_id(1) == 0)
    def _():
        for r in range(0, wbf_ref.shape[0], _CAST_ROWS):
            wbf_ref[r:r + _CAST_ROWS, :] = w_ref[r:r + _CAST_ROWS, :].astype(BF16)

    acc = (_dot_nt if transposed else _dot)(x_ref[...], wbf_ref[...])
    if scaled:
        acc = acc * scale_ref[...]
    o_ref[...] = acc.astype(o_ref.dtype)


def _in_proj(x, w3, n_out, transposed=False, skip_col=None, skip=0, col_scale=None, bm=2048, bn=1024):
    m, k = x.shape
    if transposed:
        assert skip % 8 == 0 and (skip == 0 or skip_col % bn == 0)
        start = lambda j: 8 * (j * (bn // 8) + (jnp.where(j * bn >= skip_col, skip // 8, 0) if skip else 0))
        w_spec = pl.BlockSpec((None, pl.Element(bn), pl.Element(k)), lambda j, i: (0, start(j), 0))
        w_block = (bn, k)
    else:
        assert not skip
        w_spec = pl.BlockSpec((None, k, bn), lambda j, i: (0, 0, j))
        w_block = (k, bn)
    in_specs = [pl.BlockSpec((bm, k), lambda j, i: (i, 0)), w_spec]
    args = [x, w3]
    blk = _nbytes((bm, k), BF16) + _nbytes(w_block, F32) + _nbytes((bm, bn), BF16)
    if col_scale is not None:
        in_specs.append(pl.BlockSpec((1, bn), lambda j, i: (0, j)))
        args.append(col_scale.reshape(1, n_out))
    return pl.pallas_call(
        functools.partial(_in_proj_kernel, transposed=transposed, scaled=col_scale is not None),
        grid=(n_out // bn, m // bm),
        in_specs=in_specs,
        out_specs=pl.BlockSpec((bm, bn), lambda j, i: (i, j)),
        out_shape=jax.ShapeDtypeStruct((m, n_out), BF16),
        scratch_shapes=[pltpu.VMEM(w_block, BF16)],
        compiler_params=pltpu.CompilerParams(
            dimension_semantics=("arbitrary", "arbitrary"),
            vmem_limit_bytes=_vmem_limit(blk, _nbytes(w_block, BF16), temp_bytes=_nbytes((bm, bn), F32))),
        name="in_proj",
    )(*args)


def _out_proj_norm_kernel(xa_ref, xb_ref, wa_ref, wb_ref, r_ref, nw_ref, o_ref):
    h = r_ref[...] + (_dot(xa_ref[...], wa_ref[...]) + _dot(xb_ref[...], wb_ref[...]))
    ms = jnp.mean(h * h, axis=-1, keepdims=True)
    o_ref[...] = (h * lax.rsqrt(ms + EPS) * nw_ref[...]).astype(o_ref.dtype)


def _out_proj_norm(x, w, res, norm_w, bm=512):
    m = x.shape[0]
    k = w.shape[0] // 2
    n = w.shape[1]
    blk = 2 * _nbytes((bm, k), BF16) + 2 * _nbytes((bm, n), F32)
    row = lambda i: (i, 0)
    once = pl.Buffered(1)
    return pl.pallas_call(
        _out_proj_norm_kernel,
        grid=(m // bm,),
        in_specs=[pl.BlockSpec((bm, k), row),
                  pl.BlockSpec((bm, k), lambda i: (i, 1)),
                  pl.BlockSpec((k, n), lambda i: (0, 0), pipeline_mode=once),
                  pl.BlockSpec((k, n), lambda i: (1, 0), pipeline_mode=once),
                  pl.BlockSpec((bm, n), row),
                  pl.BlockSpec((1, n), lambda i: (0, 0))],
        out_specs=pl.BlockSpec((bm, n), row),
        out_shape=jax.ShapeDtypeStruct((m, n), F32),
        compiler_params=pltpu.CompilerParams(
            dimension_semantics=("arbitrary",),
            vmem_limit_bytes=_vmem_limit(blk, 2 * _nbytes((k, n), BF16), temp_bytes=3 * _nbytes((bm, n), F32))),
        name="out_proj_norm",
    )(x, x, w, w, res, norm_w.reshape(1, n))


_CONV_TAIL = 16
_CONV_COLS = 512


def _conv_shift_matrix():
    L = SSD_CHUNK
    r = jnp.arange((SSD_CONV - 1) * L)
    src = _CONV_TAIL + (r % L) - (SSD_CONV - 1 - r // L)
    return (src[:, None] == jnp.arange(_CONV_TAIL + L)[None, :]).astype(BF16)


def _ssd_kernel(z_ref, xs_ref, bc_ref, dtr_ref, cw_ref, cb_ref, dtb_ref, alog_ref, dskip_ref,
                nw_ref, exp_ref, shift_ref, o_ref, tail_ref, xc_ref, state_ref):
    L = SSD_CHUNK
    c = pl.program_id(1)

    @pl.when(c == 0)
    def _():
        state_ref[...] = jnp.zeros_like(state_ref)
        tail_ref[...] = jnp.zeros_like(tail_ref)

    def conv_chunk(j):
        cols = slice(j * _CONV_COLS, (j + 1) * _CONV_COLS)
        src_ref, c_src = (xs_ref, j * _CONV_COLS) if j * _CONV_COLS < SSD_WIDTH else (bc_ref, j * _CONV_COLS - SSD_WIDTH)
        cur = src_ref[:, c_src:c_src + _CONV_COLS]
        shifted = _dot(shift_ref[...], jnp.concatenate([tail_ref[:, cols], cur], axis=0))
        acc = cb_ref[:, cols] + cw_ref[SSD_CONV - 1:SSD_CONV, cols] * cur.astype(F32)
        for k in range(SSD_CONV - 1):
            acc = acc + cw_ref[k:k + 1, cols] * shifted[k * L:(k + 1) * L, :]
        xc_ref[:, cols] = _silu(acc)
        tail_ref[:, cols] = cur[L - _CONV_TAIL:L, :]

    def conv_chunks_of(g):
        return (g * SSD_GROUP_WIDTH // _CONV_COLS,
                (SSD_WIDTH + g * SSD_STATE) // _CONV_COLS,
                (SSD_WIDTH + SSD_BC_WIDTH + g * SSD_STATE) // _CONV_COLS)

    conv_emitted = set()

    def conv_for(g):
        for j in conv_chunks_of(g):
            if j not in conv_emitted:
                conv_emitted.add(j)
                conv_chunk(j)

    conv_for(0)

    x_dt = dtr_ref[...] + dtb_ref[...]
    dt = jnp.maximum(x_dt, 0.0) + jnp.log1p(jnp.exp(-jnp.abs(x_dt)))
    da = dt * (-jnp.exp(alog_ref[...]))
    row = lax.broadcasted_iota(jnp.int32, (L, L), 0)
    col = lax.broadcasted_iota(jnp.int32, (L, L), 1)
    causal = col <= row
    tri = jnp.where(causal, 1.0, 0.0).astype(BF16)
    cs = sum(_dot(tri, p) for p in _split_bf16(da, 3))
    cs_t = cs.T
    cs_last = cs[L - 1:L, :]
    exp_cs = jnp.exp(cs)
    w_state = jnp.exp(cs_last - cs) * dt
    chunk_decay = jnp.broadcast_to(jnp.exp(cs_last), (8, DT_PAD))

    per_head = _split_bf16(jnp.concatenate([dt, w_state, chunk_decay], axis=0), 2)

    def expand(gcols):
        return sum(_dot(p, exp_ref[:, gcols]) for p in per_head)

    lane = lax.broadcasted_iota(jnp.int32, (2 * L, SSD_GROUP_WIDTH), 1)

    def operands(g):
        conv_for(g)
        gcols = slice(g * SSD_GROUP_WIDTH, (g + 1) * SSD_GROUP_WIDTH)
        xg = xc_ref[:, gcols]
        b0 = SSD_WIDTH + g * SSD_STATE
        c0 = SSD_WIDTH + SSD_BC_WIDTH + g * SSD_STATE
        bg = xc_ref[:, b0:b0 + SSD_STATE].astype(BF16)
        cg = xc_ref[:, c0:c0 + SSD_STATE]
        cbm = jnp.where(causal, _dot_nt(cg.astype(BF16), bg), 0.0)
        prev = state_ref[g]
        per_chan = expand(gcols)
        rhs_all = jnp.concatenate([(xg * per_chan[0:L]).astype(BF16), prev.astype(BF16)], axis=0)
        lhs_parts, rhs_parts = [], []
        for r in range(SSD_HEADS_PER_GROUP):
            h = g * SSD_HEADS_PER_GROUP + r
            seg = jnp.where(causal, cs[:, h:h + 1] - cs_t[h:h + 1, :], 0.0)
            lhs_parts.append((cbm * jnp.exp(seg)).astype(BF16))
            lhs_parts.append((cg * exp_cs[:, h:h + 1]).astype(BF16))
            in_head = (lane >= r * SSD_HEAD_DIM) & (lane < (r + 1) * SSD_HEAD_DIM)
            rhs_parts.append(jnp.where(in_head, rhs_all, jnp.zeros_like(rhs_all)))
        xs_w = (xg * per_chan[L:2 * L]).astype(BF16)
        return (xg, bg, xs_w, prev * per_chan[2 * L:2 * L + 1],
                jnp.concatenate(lhs_parts, axis=1), jnp.concatenate(rhs_parts, axis=0))

    def outputs(g, xg, bg, xs_w, decayed_state, lhs, rhs):
        gcols = slice(g * SSD_GROUP_WIDTH, (g + 1) * SSD_GROUP_WIDTH)
        y = _dot(lhs, rhs)
        y = y + xg * dskip_ref[:, gcols]
        y = y * _silu(z_ref[:, gcols].astype(F32))
        ms = jnp.mean(y * y, axis=-1, keepdims=True)
        o_ref[:, gcols] = (y * lax.rsqrt(ms + EPS) * nw_ref[:, gcols]).astype(o_ref.dtype)
        state_ref[g] = decayed_state + _dot(bg.T, xs_w)

    ops = operands(0)
    for g in range(SSD_GROUPS):
        nxt = operands(g + 1) if g + 1 < SSD_GROUPS else None
        outputs(g, *ops)
        ops = nxt


def _ssd(proj, dt_raw, conv_w, conv_b, dt_bias, a_log, d_skip, norm_w):
    bsz, s, _ = proj.shape
    L = SSD_CHUNK
    conv_dim = SSD_WIDTH + 2 * SSD_BC_WIDTH
    pad = DT_PAD - SSD_HEADS
    head_of_col = jnp.arange(SSD_WIDTH) // SSD_HEAD_DIM
    expand_mat = (jnp.arange(DT_PAD)[:, None] == head_of_col[None, :]).astype(BF16)
    blk = (3 * _nbytes((L, SSD_WIDTH), BF16) + _nbytes((L, DT_PAD), F32) + _nbytes((L, SSD_WIDTH), BF16)
           + _nbytes((SSD_CONV + 1, conv_dim), F32) + _nbytes((DT_PAD, SSD_WIDTH), BF16))
    shift_mat = _conv_shift_matrix()
    blk += _nbytes(shift_mat.shape, BF16)
    scratch = (_nbytes((_CONV_TAIL, conv_dim), BF16) + _nbytes((L, conv_dim), F32)
               + _nbytes((SSD_GROUPS, SSD_STATE, SSD_GROUP_WIDTH), F32))
    row = lambda b, c: (b, c, 0)
    const = lambda b, c: (0, 0)
    return pl.pallas_call(
        _ssd_kernel,
        grid=(bsz, s // L),
        in_specs=[pl.BlockSpec((None, L, SSD_WIDTH), lambda b, c: (b, c, 0)),
                  pl.BlockSpec((None, L, SSD_WIDTH), lambda b, c: (b, c, 1)),
                  pl.BlockSpec((None, L, SSD_WIDTH), lambda b, c: (b, c, 2)),
                  pl.BlockSpec((None, L, DT_PAD), row),
                  pl.BlockSpec((SSD_CONV, conv_dim), const),
                  pl.BlockSpec((1, conv_dim), const),
                  pl.BlockSpec((1, DT_PAD), const),
                  pl.BlockSpec((1, DT_PAD), const),
                  pl.BlockSpec((1, SSD_WIDTH), const),
                  pl.BlockSpec((1, SSD_WIDTH), const),
                  pl.BlockSpec((DT_PAD, SSD_WIDTH), const),
                  pl.BlockSpec(shift_mat.shape, const)],
        out_specs=pl.BlockSpec((None, L, SSD_WIDTH), row),
        out_shape=jax.ShapeDtypeStruct((bsz, s, SSD_WIDTH), BF16),
        scratch_shapes=[pltpu.VMEM((_CONV_TAIL, conv_dim), BF16),
                        pltpu.VMEM((L, conv_dim), F32),
                        pltpu.VMEM((SSD_GROUPS, SSD_STATE, SSD_GROUP_WIDTH), F32)],
        compiler_params=pltpu.CompilerParams(
            dimension_semantics=("arbitrary", "arbitrary"),
            vmem_limit_bytes=_vmem_limit(blk, scratch, temp_bytes=16 << 20)),
        name="ssd",
    )(proj, proj, proj, dt_raw, conv_w, conv_b.reshape(1, conv_dim),
      jnp.pad(dt_bias, (0, pad)).reshape(1, DT_PAD), jnp.pad(a_log, (0, pad)).reshape(1, DT_PAD),
      jnp.repeat(d_skip, SSD_HEAD_DIM).reshape(1, SSD_WIDTH), norm_w.reshape(1, SSD_WIDTH), expand_mat,
      shift_mat)


_SGU_CHUNKS_PER_STEP = 2
_SGU_OUT_PROJ_PIECES = 8


def _sgu_rows(zb_ref, u_ref, v_ref, lnw_ref, lnb_ref, ws_ref, bias_ref, o_ref, after_group):
    T = SGU_CHUNK
    v = _gelu_tanh(v_ref[...].astype(F32))
    mu = jnp.mean(v, axis=-1, keepdims=True)
    vc = v - mu
    var = jnp.mean(vc * vc, axis=-1, keepdims=True)
    vn = (vc * lax.rsqrt(var + EPS) * lnw_ref[...] + lnb_ref[...]).astype(BF16)
    row = lax.broadcasted_iota(jnp.int32, (T, T), 0)
    col = lax.broadcasted_iota(jnp.int32, (T, T), 1)
    mixed_all = []
    for g in range(SGU_GROUPS):
        gcols = slice(g * SGU_GROUP_DIM, (g + 1) * SGU_GROUP_DIM)
        ws = jnp.where(col <= row, ws_ref[g], 0.0).astype(BF16)
        rhs = jnp.concatenate([vn[k * T:(k + 1) * T, gcols] for k in range(_SGU_CHUNKS_PER_STEP)], axis=1)
        mixed_all.append(_dot(ws, rhs) + bias_ref[:, g:g + 1])
    for g in range(SGU_GROUPS):
        gcols = slice(g * SGU_GROUP_DIM, (g + 1) * SGU_GROUP_DIM)
        mixed = mixed_all[g]
        for k in range(_SGU_CHUNKS_PER_STEP):
            rows = slice(k * T, (k + 1) * T)
            u = _gelu_tanh(u_ref[rows, gcols].astype(F32))
            gate = _silu(zb_ref[rows, gcols].astype(F32))
            o_ref[rows, gcols] = (u * mixed[:, k * SGU_GROUP_DIM:(k + 1) * SGU_GROUP_DIM] * gate).astype(o_ref.dtype)
        after_group(g)


def _sgu_out_proj_kernel(ya_ref, zb_ref, u_ref, v_ref, lnw_ref, lnb_ref, ws_ref, bias_ref, wa_ref, wb_ref,
                         r_ref, nw_ref, h_ref, hn_ref, yb_ref):
    cw = h_ref.shape[1] // _SGU_OUT_PROJ_PIECES

    def after_group(g):
        for c in range(g * _SGU_OUT_PROJ_PIECES // SGU_GROUPS, (g + 1) * _SGU_OUT_PROJ_PIECES // SGU_GROUPS):
            cc = slice(c * cw, (c + 1) * cw)
            h_ref[:, cc] = r_ref[:, cc] + _dot(ya_ref[...], wa_ref[:, cc])

    _sgu_rows(zb_ref, u_ref, v_ref, lnw_ref, lnb_ref, ws_ref, bias_ref, yb_ref, after_group)
    h = h_ref[...] + _dot(yb_ref[...], wb_ref[...])
    h_ref[...] = h
    ms = jnp.mean(h * h, axis=-1, keepdims=True)
    hn_ref[...] = (h * lax.rsqrt(ms + EPS) * nw_ref[...]).astype(hn_ref.dtype)


def _sgu_out_proj(ya, proj2d, col0, ln_w, ln_b, ws, bias, w, res, norm_w):
    m, k = ya.shape
    n = w.shape[1]
    bm = _SGU_CHUNKS_PER_STEP * SGU_CHUNK
    blk = (4 * _nbytes((bm, k), BF16) + _nbytes((SGU_GROUPS, SGU_CHUNK, SGU_CHUNK), F32)
           + 2 * _nbytes((bm, n), F32) + _nbytes((bm, n), BF16))
    row = lambda i: (i, 0)
    const2 = lambda i: (0, 0)
    once = pl.Buffered(1)
    return pl.pallas_call(
        _sgu_out_proj_kernel,
        grid=(m // bm,),
        in_specs=[pl.BlockSpec((bm, k), row),
                  pl.BlockSpec((bm, SGU_WIDTH), lambda i: (i, col0)),
                  pl.BlockSpec((bm, SGU_WIDTH), lambda i: (i, col0 + 1)),
                  pl.BlockSpec((bm, SGU_WIDTH), lambda i: (i, col0 + 2)),
                  pl.BlockSpec((1, SGU_WIDTH), const2),
                  pl.BlockSpec((1, SGU_WIDTH), const2),
                  pl.BlockSpec((SGU_GROUPS, SGU_CHUNK, SGU_CHUNK), lambda i: (0, 0, 0)),
                  pl.BlockSpec((SGU_CHUNK, SGU_GROUPS), const2),
                  pl.BlockSpec((k, n), lambda i: (0, 0), pipeline_mode=once),
                  pl.BlockSpec((SGU_WIDTH, n), lambda i: (k // SGU_WIDTH, 0), pipeline_mode=once),
                  pl.BlockSpec((bm, n), row),
                  pl.BlockSpec((1, n), const2)],
        out_specs=[pl.BlockSpec((bm, n), row), pl.BlockSpec((bm, n), row)],
        out_shape=[jax.ShapeDtypeStruct((m, n), F32), jax.ShapeDtypeStruct((m, n), BF16)],
        scratch_shapes=[pltpu.VMEM((bm, SGU_WIDTH), BF16)],
        compiler_params=pltpu.CompilerParams(
            dimension_semantics=("arbitrary",),
            vmem_limit_bytes=_vmem_limit(blk, _nbytes((k + SGU_WIDTH, n), BF16) + _nbytes((bm, SGU_WIDTH), BF16),
                                         temp_bytes=8 * _nbytes((bm, SGU_WIDTH), F32))),
        name="sgu_out_proj",
    )(ya, proj2d, proj2d, proj2d, ln_w.reshape(1, SGU_WIDTH), ln_b.reshape(1, SGU_WIDTH), ws, bias.T,
      w, w, res, norm_w.reshape(1, n))


_ATTN_BQ = 256
_ATTN_BK = 256
_MASKED = -1e30


ATTN_Q_SCALE = DIFF_HEAD_DIM ** -0.5 * math.log2(math.e)


def _softmax_block(st):
    m = jnp.max(st, axis=0, keepdims=True)
    e = jnp.exp2(st - m)
    return e, m, jnp.sum(e, axis=0, keepdims=True)


def _attn_kernel(q_ref, k_ref, v_ref, g_ref, lq1_ref, lk1_ref, lq2_ref, lk2_ref, sw_ref, o_ref, *,
                 lambda_init):
    s_len = q_ref.shape[0]
    d = DIFF_HEAD_DIM
    bq, bk = _ATTN_BQ, _ATTN_BK
    lam = (jnp.exp(jnp.sum(lq1_ref[...] * lk1_ref[...], axis=-1, keepdims=True))
           - jnp.exp(jnp.sum(lq2_ref[...] * lk2_ref[...], axis=-1, keepdims=True)) + lambda_init)
    key_idx = lax.broadcasted_iota(jnp.int32, (bk, bq), 0)
    query_idx = lax.broadcasted_iota(jnp.int32, (bk, bq), 1)
    v_t = v_ref[...].astype(F32).T.astype(BF16)
    out_scale = sw_ref[...] * (1.0 - lambda_init)

    def combine(p_blocks, t, weights):
        (e1, f1), (e2, f2) = weights[0][t], weights[1][t]
        return (p_blocks or []) + [(e1 * f1 - e2 * f2).astype(BF16)]

    def finish(lo, hi, p_blocks):
        o = _dot(v_t[:, 0:hi], jnp.concatenate(p_blocks, axis=0)).T
        ms = jnp.mean(o * o, axis=-1, keepdims=True)
        o = o * lax.rsqrt(ms + EPS) * out_scale
        o_ref[lo:hi, :] = (o * _silu(g_ref[lo:hi, :].astype(F32))).astype(o_ref.dtype)

    n_tiles = s_len // bq
    pending = None
    for i in list(range(1, n_tiles)) + [0]:
        lo, hi = i * bq, (i + 1) * bq
        q = q_ref[lo:hi, :]
        n_blocks = hi // bk
        n_comb = 0 if pending is None else len(pending[2][0])
        blocks = ([], [])
        o_t = None
        for t in range(n_blocks):
            for j in range(2):
                blk = _dot_nt(k_ref[t * bk:(t + 1) * bk, j * d:(j + 1) * d], q[:, j * d:(j + 1) * d])
                if (t + 1) * bk - 1 > lo:
                    blk = jnp.where(key_idx + (t * bk - lo) <= query_idx, blk, _MASKED)
                blocks[j].append(_softmax_block(blk))
            for tc in range(t * n_comb // n_blocks, (t + 1) * n_comb // n_blocks):
                o_t = combine(o_t, tc, pending[2])
        if pending is not None:
            finish(pending[0], pending[1], o_t)
        weights = []
        for j in range(2):
            m_all = functools.reduce(jnp.maximum, [b[1] for b in blocks[j]])
            alphas = [jnp.exp2(b[1] - m_all) for b in blocks[j]]
            total = sum(a * b[2] for a, b in zip(alphas, blocks[j]))
            norm = (1.0 / total) if j == 0 else (lam / total)
            weights.append([(b[0], a * norm) for a, b in zip(alphas, blocks[j])])
        pending = (lo, hi, weights)
    o_t = None
    for t in range(len(pending[2][0])):
        o_t = combine(o_t, t, pending[2])
    finish(pending[0], pending[1], o_t)


def _diff_attention(proj, lq1, lk1, lq2, lk2, subln_w, lambda_init):
    bsz, s, _ = proj.shape
    blk = 5 * _nbytes((s, DIFF_V_DIM), BF16)
    vec = lambda a: a.reshape(1, -1)
    const = lambda b, h: (0, 0)
    return pl.pallas_call(
        functools.partial(_attn_kernel, lambda_init=lambda_init),
        grid=(bsz, DIFF_HEADS),
        in_specs=[pl.BlockSpec((None, s, DIFF_V_DIM), lambda b, h: (b, 0, h)),
                  pl.BlockSpec((None, s, DIFF_V_DIM), lambda b, h: (b, 0, DIFF_HEADS + h)),
                  pl.BlockSpec((None, s, DIFF_V_DIM), lambda b, h: (b, 0, 2 * DIFF_HEADS + h)),
                  pl.BlockSpec((None, s, DIFF_V_DIM), lambda b, h: (b, 0, 3 * DIFF_HEADS + h)),
                  pl.BlockSpec((1, DIFF_HEAD_DIM), const),
                  pl.BlockSpec((1, DIFF_HEAD_DIM), const),
                  pl.BlockSpec((1, DIFF_HEAD_DIM), const),
                  pl.BlockSpec((1, DIFF_HEAD_DIM), const),
                  pl.BlockSpec((1, DIFF_V_DIM), const)],
        out_specs=pl.BlockSpec((None, s, DIFF_V_DIM), lambda b, h: (b, 0, h)),
        out_shape=jax.ShapeDtypeStruct((bsz, s, DIFF_WIDTH), BF16),
        compiler_params=pltpu.CompilerParams(
            dimension_semantics=("arbitrary", "arbitrary"),
            vmem_limit_bytes=_vmem_limit(blk, temp_bytes=8 * _nbytes((_ATTN_BQ, s), F32))),
        name="diff_attn",
    )(proj, proj, proj, proj, vec(lq1), vec(lk1), vec(lq2), vec(lk2), vec(subln_w))


def kernel(x, norm_w, even_w_in, even_conv_w, even_conv_b, even_dt_bias, even_a_log, even_d_skip,
           even_ssd_norm_w, even_sgu_ln_w, even_sgu_ln_b, even_sgu_ws, even_sgu_b, even_w_out,
           odd_w_in, odd_lam_q1, odd_lam_k1, odd_lam_q2, odd_lam_k2, odd_subln_w, odd_w_out,
           final_norm_w):
    bsz, s, d = x.shape
    m = bsz * s
    x2 = x.reshape(m, d)

    w_in_t = jnp.swapaxes(even_w_in, 1, 2)
    hn, dt_raw = _rmsnorm_dt(x2, norm_w[0], w_in_t)
    n_main = even_w_in.shape[2] - SSD_HEADS
    proj = _in_proj(hn, w_in_t, n_main, transposed=True, skip_col=SSD_DT_COL,
                    skip=SSD_HEADS)
    y_a = _ssd(proj.reshape(bsz, s, -1), dt_raw.reshape(bsz, s, DT_PAD), even_conv_w[0], even_conv_b[0],
               even_dt_bias[0], even_a_log[0], even_d_skip[0], even_ssd_norm_w[0])
    h1, hn1 = _sgu_out_proj(y_a.reshape(m, SSD_WIDTH), proj, 3, even_sgu_ln_w[0], even_sgu_ln_b[0],
                            even_sgu_ws[0], even_sgu_b[0], even_w_out[0].astype(BF16), x2, norm_w[1])

    lambda_init = 0.8 - 0.6 * math.exp(-0.3 * 1)
    n_in1 = odd_w_in.shape[2]
    q_scale = jnp.where(jnp.arange(n_in1) < DIFF_WIDTH, ATTN_Q_SCALE, 1.0).astype(F32)
    proj1 = _in_proj(hn1, odd_w_in, n_in1, col_scale=q_scale)
    o = _diff_attention(proj1.reshape(bsz, s, -1), odd_lam_q1[0], odd_lam_k1[0], odd_lam_q2[0],
                        odd_lam_k2[0], odd_subln_w[0], lambda_init)
    out = _out_proj_norm(o.reshape(m, DIFF_WIDTH), odd_w_out[0].astype(BF16), h1, final_norm_w)
    return out.reshape(bsz, s, d)
```

```python
import functools
import math

import jax
import jax.numpy as jnp
from jax import lax
from jax.experimental import pallas as pl
from jax.experimental.pallas import tpu as pltpu

F32 = jnp.float32
BF16 = jnp.bfloat16

D_MODEL = 2048
SSD_WIDTH = 2048
SSD_HEAD_DIM = 64
SSD_HEADS = 32
SSD_GROUPS = 8
SSD_HEADS_PER_GROUP = SSD_HEADS // SSD_GROUPS
SSD_GROUP_WIDTH = SSD_WIDTH // SSD_GROUPS
SSD_STATE = 128
SSD_CONV = 4
SSD_CHUNK = 128
SSD_BC_WIDTH = SSD_GROUPS * SSD_STATE
SSD_DT_COL = 2 * SSD_WIDTH + 2 * SSD_BC_WIDTH
SGU_WIDTH = 2048
SGU_CHUNK = 128
SGU_GROUPS = 16
SGU_GROUP_DIM = 128
DIFF_HEADS = 16
DIFF_HEAD_DIM = 128
DIFF_V_DIM = 256
DIFF_WIDTH = DIFF_HEADS * DIFF_V_DIM
EPS = 1e-6

V7X_LANES = 128
V7X_VMEM_BYTES = 64 * 1024 * 1024
DT_PAD = V7X_LANES


def _vmem_limit(block_bytes, scratch_bytes=0, temp_bytes=0):
    est = 2 * block_bytes + scratch_bytes + temp_bytes + (4 << 20)
    return int(min(est, V7X_VMEM_BYTES - (8 << 20)))


def _nbytes(shape, dtype):
    return math.prod(shape) * jnp.dtype(dtype).itemsize


def _silu(x):
    h = 0.5 * x
    return h + h * jnp.tanh(h)


def _gelu_tanh(x):
    c = math.sqrt(2.0 / math.pi)
    h = 0.5 * x
    return h + h * jnp.tanh(x * (c + (c * 0.044715) * (x * x)))


def _split_bf16(x, n):
    parts = []
    r = x
    for i in range(n):
        p = r.astype(BF16)
        parts.append(p)
        if i + 1 < n:
            r = r - p.astype(F32)
    return parts


def _dot(a, b):
    return jnp.dot(a, b, preferred_element_type=F32)


def _dot_nt(a, b):
    return lax.dot_general(a, b, (((1,), (1,)), ((), ())), preferred_element_type=F32)


def _rms_dt_kernel(x_ref, w_ref, wdt_ref, o_ref, dt_ref):
    x = x_ref[...]
    ms = jnp.mean(x * x, axis=-1, keepdims=True)
    y = x * lax.rsqrt(ms + EPS) * w_ref[...]
    y_hi, y_lo = _split_bf16(y, 2)
    o_ref[...] = y_hi
    row = lax.broadcasted_iota(jnp.int32, wdt_ref.shape, 0)
    w_hi, w_lo = _split_bf16(jnp.where(row < SSD_HEADS, wdt_ref[...], 0.0), 2)
    dt_ref[...] = _dot_nt(y_hi, w_hi) + (_dot_nt(y_lo, w_hi) + _dot_nt(y_hi, w_lo))


def _rmsnorm_dt(x, w, w_in_t, bm=512):
    m, d = x.shape
    blk = _nbytes((bm, d), F32) + _nbytes((bm, d), BF16) + _nbytes((DT_PAD, d), F32)
    return pl.pallas_call(
        _rms_dt_kernel,
        grid=(m // bm,),
        in_specs=[pl.BlockSpec((bm, d), lambda i: (i, 0)),
                  pl.BlockSpec((1, d), lambda i: (0, 0)),
                  pl.BlockSpec((None, DT_PAD, d), lambda i: (0, SSD_DT_COL // DT_PAD, 0))],
        out_specs=[pl.BlockSpec((bm, d), lambda i: (i, 0)),
                   pl.BlockSpec((bm, DT_PAD), lambda i: (i, 0))],
        out_shape=[jax.ShapeDtypeStruct((m, d), BF16),
                   jax.ShapeDtypeStruct((m, DT_PAD), F32)],
        compiler_params=pltpu.CompilerParams(
            dimension_semantics=("arbitrary",),
            vmem_limit_bytes=_vmem_limit(blk, temp_bytes=4 * _nbytes((bm, d), F32))),
        name="rmsnorm_dt",
    )(x, w.reshape(1, d), w_in_t)


_CAST_ROWS = 256


def _in_proj_kernel(x_ref, w_ref, *rest, transposed, scaled):
    rest = list(rest)
    scale_ref = rest.pop(0) if scaled else None
    o_ref, wbf_ref = rest

    first = pl.program_id(1) == 0

    def project(w):
        acc = (_dot_nt if transposed else _dot)(x_ref[...], w)
        if scaled:
            acc = acc * scale_ref[...]
        o_ref[...] = acc.astype(o_ref.dtype)

    @pl.when(first)
    def _():
        w = w_ref[...].astype(BF16)
        wbf_ref[...] = w
        project(w)

    @pl.when(jnp.logical_not(first))
    def _():
        project(wbf_ref[...])


def _in_proj(x, w3, n_out, transposed=False, skip_col=None, skip=0, col_scale=None, bm=2048, bn=1024):
    m, k = x.shape
    if transposed:
        assert skip % 8 == 0 and (skip == 0 or skip_col % bn == 0)
        start = lambda j: 8 * (j * (bn // 8) + (jnp.where(j * bn >= skip_col, skip // 8, 0) if skip else 0))
        w_spec = pl.BlockSpec((None, pl.Element(bn), pl.Element(k)), lambda j, i: (0, start(j), 0))
        w_block = (bn, k)
    else:
        assert not skip
        w_spec = pl.BlockSpec((None, k, bn), lambda j, i: (0, 0, j))
        w_block = (k, bn)
    in_specs = [pl.BlockSpec((bm, k), lambda j, i: (i, 0)), w_spec]
    args = [x, w3]
    blk = _nbytes((bm, k), BF16) + _nbytes(w_block, F32) + _nbytes((bm, bn), BF16)
    if col_scale is not None:
        in_specs.append(pl.BlockSpec((1, bn), lambda j, i: (0, j)))
        args.append(col_scale.reshape(1, n_out))
    return pl.pallas_call(
        functools.partial(_in_proj_kernel, transposed=transposed, scaled=col_scale is not None),
        grid=(n_out // bn, m // bm),
        in_specs=in_specs,
        out_specs=pl.BlockSpec((bm, bn), lambda j, i: (i, j)),
        out_shape=jax.ShapeDtypeStruct((m, n_out), BF16),
        scratch_shapes=[pltpu.VMEM(w_block, BF16)],
        compiler_params=pltpu.CompilerParams(
            dimension_semantics=("arbitrary", "arbitrary"),
            vmem_limit_bytes=_vmem_limit(blk, _nbytes(w_block, BF16), temp_bytes=_nbytes((bm, bn), F32))),
        name="in_proj",
    )(*args)


def _out_proj_norm_kernel(xa_ref, xb_ref, wa_ref, wb_ref, r_ref, nw_ref, o_ref):
    h = r_ref[...] + (_dot(xa_ref[...], wa_ref[...]) + _dot(xb_ref[...], wb_ref[...]))
    ms = jnp.mean(h * h, axis=-1, keepdims=True)
    o_ref[...] = (h * lax.rsqrt(ms + EPS) * nw_ref[...]).astype(o_ref.dtype)


def _out_proj_norm(x, w, res, norm_w, bm=512):
    m = x.shape[0]
    k = w.shape[0] // 2
    n = w.shape[1]
    blk = 2 * _nbytes((bm, k), BF16) + 2 * _nbytes((bm, n), F32)
    row = lambda i: (i, 0)
    once = pl.Buffered(1)
    return pl.pallas_call(
        _out_proj_norm_kernel,
        grid=(m // bm,),
        in_specs=[pl.BlockSpec((bm, k), row),
                  pl.BlockSpec((bm, k), lambda i: (i, 1)),
                  pl.BlockSpec((k, n), lambda i: (0, 0), pipeline_mode=once),
                  pl.BlockSpec((k, n), lambda i: (1, 0), pipeline_mode=once),
                  pl.BlockSpec((bm, n), row),
                  pl.BlockSpec((1, n), lambda i: (0, 0))],
        out_specs=pl.BlockSpec((bm, n), row),
        out_shape=jax.ShapeDtypeStruct((m, n), F32),
        compiler_params=pltpu.CompilerParams(
            dimension_semantics=("arbitrary",),
            vmem_limit_bytes=_vmem_limit(blk, 2 * _nbytes((k, n), BF16), temp_bytes=3 * _nbytes((bm, n), F32))),
        name="out_proj_norm",
    )(x, x, w, w, res, norm_w.reshape(1, n))


_CONV_TAIL = 16
_CONV_COLS = 512


def _conv_shift_matrix():
    L = SSD_CHUNK
    r = jnp.arange((SSD_CONV - 1) * L)
    src = _CONV_TAIL + (r % L) - (SSD_CONV - 1 - r // L)
    return (src[:, None] == jnp.arange(_CONV_TAIL + L)[None, :]).astype(BF16)


_SSD_CHUNKS_PER_STEP = 4


def _ssd_kernel(z_ref, xs_ref, bc_ref, dtr_ref, cw_ref, cb_ref, dtb_ref, alog_ref, dskip_ref,
                nw_ref, exp_ref, shift_ref, o_ref, tail_ref, xc_ref, state_ref):
    @pl.when(pl.program_id(1) == 0)
    def _():
        state_ref[...] = jnp.zeros_like(state_ref)
        tail_ref[...] = jnp.zeros_like(tail_ref)

    for sub in range(_SSD_CHUNKS_PER_STEP):
        rows = pl.ds(sub * SSD_CHUNK, SSD_CHUNK)
        _ssd_chunk(z_ref.at[rows], xs_ref.at[rows], bc_ref.at[rows], dtr_ref.at[rows], cw_ref, cb_ref, dtb_ref,
                   alog_ref, dskip_ref, nw_ref, exp_ref, shift_ref, o_ref.at[rows], tail_ref, xc_ref.at[sub],
                   state_ref)


def _ssd_chunk(z_ref, xs_ref, bc_ref, dtr_ref, cw_ref, cb_ref, dtb_ref, alog_ref, dskip_ref,
               nw_ref, exp_ref, shift_ref, o_ref, tail_ref, xc_ref, state_ref):
    L = SSD_CHUNK

    def conv_chunk(j):
        cols = slice(j * _CONV_COLS, (j + 1) * _CONV_COLS)
        src_ref, c_src = (xs_ref, j * _CONV_COLS) if j * _CONV_COLS < SSD_WIDTH else (bc_ref, j * _CONV_COLS - SSD_WIDTH)
        cur = src_ref[:, c_src:c_src + _CONV_COLS]
        shifted = _dot(shift_ref[...], jnp.concatenate([tail_ref[:, cols], cur], axis=0))
        acc = cb_ref[:, cols] + cw_ref[SSD_CONV - 1:SSD_CONV, cols] * cur.astype(F32)
        for k in range(SSD_CONV - 1):
            acc = acc + cw_ref[k:k + 1, cols] * shifted[k * L:(k + 1) * L, :]
        xc_ref[:, cols] = _silu(acc)
        tail_ref[:, cols] = cur[L - _CONV_TAIL:L, :]

    def conv_chunks_of(g):
        return (g * SSD_GROUP_WIDTH // _CONV_COLS,
                (SSD_WIDTH + g * SSD_STATE) // _CONV_COLS,
                (SSD_WIDTH + SSD_BC_WIDTH + g * SSD_STATE) // _CONV_COLS)

    conv_emitted = set()

    def conv_for(g):
        for j in conv_chunks_of(g):
            if j not in conv_emitted:
                conv_emitted.add(j)
                conv_chunk(j)

    conv_for(0)

    x_dt = dtr_ref[...] + dtb_ref[...]
    dt = jnp.maximum(x_dt, 0.0) + jnp.log1p(jnp.exp(-jnp.abs(x_dt)))
    da = dt * (-jnp.exp(alog_ref[...]))
    row = lax.broadcasted_iota(jnp.int32, (L, L), 0)
    col = lax.broadcasted_iota(jnp.int32, (L, L), 1)
    causal = col <= row
    tri = jnp.where(causal, 1.0, 0.0).astype(BF16)
    cs = sum(_dot(tri, p) for p in _split_bf16(da, 3))
    cs_t = cs.T
    cs_last = cs[L - 1:L, :]
    exp_cs = jnp.exp(cs)
    w_state = jnp.exp(cs_last - cs) * dt
    chunk_decay = jnp.broadcast_to(jnp.exp(cs_last), (8, DT_PAD))

    per_head = _split_bf16(jnp.concatenate([dt, w_state, chunk_decay], axis=0), 2)

    def expand(gcols):
        return sum(_dot(p, exp_ref[:, gcols]) for p in per_head)

    lane = lax.broadcasted_iota(jnp.int32, (2 * L, SSD_GROUP_WIDTH), 1)

    def operands(g):
        conv_for(g)
        gcols = slice(g * SSD_GROUP_WIDTH, (g + 1) * SSD_GROUP_WIDTH)
        xg = xc_ref[:, gcols]
        b0 = SSD_WIDTH + g * SSD_STATE
        c0 = SSD_WIDTH + SSD_BC_WIDTH + g * SSD_STATE
        bg = xc_ref[:, b0:b0 + SSD_STATE].astype(BF16)
        cg = xc_ref[:, c0:c0 + SSD_STATE]
        cbm = jnp.where(causal, _dot_nt(cg.astype(BF16), bg), 0.0)
        prev = state_ref[g]
        per_chan = expand(gcols)
        rhs_all = jnp.concatenate([(xg * per_chan[0:L]).astype(BF16), prev.astype(BF16)], axis=0)
        lhs_parts, rhs_parts = [], []
        for r in range(SSD_HEADS_PER_GROUP):
            h = g * SSD_HEADS_PER_GROUP + r
            seg = jnp.where(causal, cs[:, h:h + 1] - cs_t[h:h + 1, :], 0.0)
            lhs_parts.append((cbm * jnp.exp(seg)).astype(BF16))
            lhs_parts.append((cg * exp_cs[:, h:h + 1]).astype(BF16))
            in_head = (lane >= r * SSD_HEAD_DIM) & (lane < (r + 1) * SSD_HEAD_DIM)
            rhs_parts.append(jnp.where(in_head, rhs_all, jnp.zeros_like(rhs_all)))
        xs_w = (xg * per_chan[L:2 * L]).astype(BF16)
        return (xg, bg, xs_w, prev * per_chan[2 * L:2 * L + 1],
                jnp.concatenate(lhs_parts, axis=1), jnp.concatenate(rhs_parts, axis=0))

    def outputs(g, xg, bg, xs_w, decayed_state, lhs, rhs):
        gcols = slice(g * SSD_GROUP_WIDTH, (g + 1) * SSD_GROUP_WIDTH)
        y = _dot(lhs, rhs)
        y = y + xg * dskip_ref[:, gcols]
        y = y * _silu(z_ref[:, gcols].astype(F32))
        ms = jnp.mean(y * y, axis=-1, keepdims=True)
        o_ref[:, gcols] = (y * lax.rsqrt(ms + EPS) * nw_ref[:, gcols]).astype(o_ref.dtype)
        state_ref[g] = decayed_state + _dot(bg.T, xs_w)

    ops = operands(0)
    for g in range(SSD_GROUPS):
        nxt = operands(g + 1) if g + 1 < SSD_GROUPS else None
        outputs(g, *ops)
        ops = nxt


def _ssd(proj, dt_raw, conv_w, conv_b, dt_bias, a_log, d_skip, norm_w):
    bsz, s, _ = proj.shape
    L = _SSD_CHUNKS_PER_STEP * SSD_CHUNK
    conv_dim = SSD_WIDTH + 2 * SSD_BC_WIDTH
    pad = DT_PAD - SSD_HEADS
    head_of_col = jnp.arange(SSD_WIDTH) // SSD_HEAD_DIM
    expand_mat = (jnp.arange(DT_PAD)[:, None] == head_of_col[None, :]).astype(BF16)
    blk = (3 * _nbytes((L, SSD_WIDTH), BF16) + _nbytes((L, DT_PAD), F32) + _nbytes((L, SSD_WIDTH), BF16)
           + _nbytes((SSD_CONV + 1, conv_dim), F32) + _nbytes((DT_PAD, SSD_WIDTH), BF16))
    shift_mat = _conv_shift_matrix()
    blk += _nbytes(shift_mat.shape, BF16)
    scratch = (_nbytes((_CONV_TAIL, conv_dim), BF16) + _nbytes((L, conv_dim), F32)
               + _nbytes((SSD_GROUPS, SSD_STATE, SSD_GROUP_WIDTH), F32))
    row = lambda b, c: (b, c, 0)
    const = lambda b, c: (0, 0)
    return pl.pallas_call(
        _ssd_kernel,
        grid=(bsz, s // L),
        in_specs=[pl.BlockSpec((None, L, SSD_WIDTH), lambda b, c: (b, c, 0)),
                  pl.BlockSpec((None, L, SSD_WIDTH), lambda b, c: (b, c, 1)),
                  pl.BlockSpec((None, L, SSD_WIDTH), lambda b, c: (b, c, 2)),
                  pl.BlockSpec((None, L, DT_PAD), row),
                  pl.BlockSpec((SSD_CONV, conv_dim), const),
                  pl.BlockSpec((1, conv_dim), const),
                  pl.BlockSpec((1, DT_PAD), const),
                  pl.BlockSpec((1, DT_PAD), const),
                  pl.BlockSpec((1, SSD_WIDTH), const),
                  pl.BlockSpec((1, SSD_WIDTH), const),
                  pl.BlockSpec((DT_PAD, SSD_WIDTH), const),
                  pl.BlockSpec(shift_mat.shape, const)],
        out_specs=pl.BlockSpec((None, L, SSD_WIDTH), row),
        out_shape=jax.ShapeDtypeStruct((bsz, s, SSD_WIDTH), BF16),
        scratch_shapes=[pltpu.VMEM((_CONV_TAIL, conv_dim), BF16),
                        pltpu.VMEM((_SSD_CHUNKS_PER_STEP, SSD_CHUNK, conv_dim), F32),
                        pltpu.VMEM((SSD_GROUPS, SSD_STATE, SSD_GROUP_WIDTH), F32)],
        compiler_params=pltpu.CompilerParams(
            dimension_semantics=("arbitrary", "arbitrary"),
            vmem_limit_bytes=_vmem_limit(blk, scratch, temp_bytes=16 << 20)),
        name="ssd",
    )(proj, proj, proj, dt_raw, conv_w, conv_b.reshape(1, conv_dim),
      jnp.pad(dt_bias, (0, pad)).reshape(1, DT_PAD), jnp.pad(a_log, (0, pad)).reshape(1, DT_PAD),
      jnp.repeat(d_skip, SSD_HEAD_DIM).reshape(1, SSD_WIDTH), norm_w.reshape(1, SSD_WIDTH), expand_mat,
      shift_mat)


_SGU_CHUNKS_PER_STEP = 2
_SGU_OUT_PROJ_PIECES = 8


def _sgu_rows(zb_ref, u_ref, v_ref, lnw_ref, lnb_ref, ws_ref, bias_ref, o_ref, after_group):
    T = SGU_CHUNK
    v = _gelu_tanh(v_ref[...].astype(F32))
    mu = jnp.mean(v, axis=-1, keepdims=True)
    vc = v - mu
    var = jnp.mean(vc * vc, axis=-1, keepdims=True)
    vn = (vc * lax.rsqrt(var + EPS) * lnw_ref[...] + lnb_ref[...]).astype(BF16)
    row = lax.broadcasted_iota(jnp.int32, (T, T), 0)
    col = lax.broadcasted_iota(jnp.int32, (T, T), 1)
    mixed_all = []
    for g in range(SGU_GROUPS):
        gcols = slice(g * SGU_GROUP_DIM, (g + 1) * SGU_GROUP_DIM)
        ws = jnp.where(col <= row, ws_ref[g], 0.0).astype(BF16)
        rhs = jnp.concatenate([vn[k * T:(k + 1) * T, gcols] for k in range(_SGU_CHUNKS_PER_STEP)], axis=1)
        mixed_all.append(_dot(ws, rhs) + bias_ref[:, g:g + 1])
    for g in range(SGU_GROUPS):
        gcols = slice(g * SGU_GROUP_DIM, (g + 1) * SGU_GROUP_DIM)
        mixed = mixed_all[g]
        for k in range(_SGU_CHUNKS_PER_STEP):
            rows = slice(k * T, (k + 1) * T)
            u = _gelu_tanh(u_ref[rows, gcols].astype(F32))
            gate = _silu(zb_ref[rows, gcols].astype(F32))
            o_ref[rows, gcols] = (u * mixed[:, k * SGU_GROUP_DIM:(k + 1) * SGU_GROUP_DIM] * gate).astype(o_ref.dtype)
        after_group(g)


def _sgu_out_proj_kernel(ya_ref, zb_ref, u_ref, v_ref, lnw_ref, lnb_ref, ws_ref, bias_ref, wa_ref, wb_ref,
                         r_ref, nw_ref, h_ref, hn_ref, yb_ref):
    cw = h_ref.shape[1] // _SGU_OUT_PROJ_PIECES

    def after_group(g):
        for c in range(g * _SGU_OUT_PROJ_PIECES // SGU_GROUPS, (g + 1) * _SGU_OUT_PROJ_PIECES // SGU_GROUPS):
            cc = slice(c * cw, (c + 1) * cw)
            h_ref[:, cc] = r_ref[:, cc] + _dot(ya_ref[...], wa_ref[:, cc])

    _sgu_rows(zb_ref, u_ref, v_ref, lnw_ref, lnb_ref, ws_ref, bias_ref, yb_ref, after_group)
    h = h_ref[...] + _dot(yb_ref[...], wb_ref[...])
    h_ref[...] = h
    ms = jnp.mean(h * h, axis=-1, keepdims=True)
    hn_ref[...] = (h * lax.rsqrt(ms + EPS) * nw_ref[...]).astype(hn_ref.dtype)


def _sgu_out_proj(ya, proj2d, col0, ln_w, ln_b, ws, bias, w, res, norm_w):
    m, k = ya.shape
    n = w.shape[1]
    bm = _SGU_CHUNKS_PER_STEP * SGU_CHUNK
    blk = (4 * _nbytes((bm, k), BF16) + _nbytes((SGU_GROUPS, SGU_CHUNK, SGU_CHUNK), F32)
           + 2 * _nbytes((bm, n), F32) + _nbytes((bm, n), BF16))
    row = lambda i: (i, 0)
    const2 = lambda i: (0, 0)
    once = pl.Buffered(1)
    return pl.pallas_call(
        _sgu_out_proj_kernel,
        grid=(m // bm,),
        in_specs=[pl.BlockSpec((bm, k), row),
                  pl.BlockSpec((bm, SGU_WIDTH), lambda i: (i, col0)),
                  pl.BlockSpec((bm, SGU_WIDTH), lambda i: (i, col0 + 1)),
                  pl.BlockSpec((bm, SGU_WIDTH), lambda i: (i, col0 + 2)),
                  pl.BlockSpec((1, SGU_WIDTH), const2),
                  pl.BlockSpec((1, SGU_WIDTH), const2),
                  pl.BlockSpec((SGU_GROUPS, SGU_CHUNK, SGU_CHUNK), lambda i: (0, 0, 0)),
                  pl.BlockSpec((SGU_CHUNK, SGU_GROUPS), const2),
                  pl.BlockSpec((k, n), lambda i: (0, 0), pipeline_mode=once),
                  pl.BlockSpec((SGU_WIDTH, n), lambda i: (k // SGU_WIDTH, 0), pipeline_mode=once),
                  pl.BlockSpec((bm, n), row),
                  pl.BlockSpec((1, n), const2)],
        out_specs=[pl.BlockSpec((bm, n), row), pl.BlockSpec((bm, n), row)],
        out_shape=[jax.ShapeDtypeStruct((m, n), F32), jax.ShapeDtypeStruct((m, n), BF16)],
        scratch_shapes=[pltpu.VMEM((bm, SGU_WIDTH), BF16)],
        compiler_params=pltpu.CompilerParams(
            dimension_semantics=("arbitrary",),
            vmem_limit_bytes=_vmem_limit(blk, _nbytes((k + SGU_WIDTH, n), BF16) + _nbytes((bm, SGU_WIDTH), BF16),
                                         temp_bytes=8 * _nbytes((bm, SGU_WIDTH), F32))),
        name="sgu_out_proj",
    )(ya, proj2d, proj2d, proj2d, ln_w.reshape(1, SGU_WIDTH), ln_b.reshape(1, SGU_WIDTH), ws, bias.T,
      w, w, res, norm_w.reshape(1, n))


_ATTN_BQ = 256
_ATTN_BK = 256
_ATTN_HEADS_PER_STEP = 2
_MASKED = -1e30


ATTN_Q_SCALE = DIFF_HEAD_DIM ** -0.5 * math.log2(math.e)


def _softmax_block(st):
    m = jnp.max(st, axis=0, keepdims=True)
    e = jnp.exp2(st - m)
    return e, m, jnp.sum(e, axis=0, keepdims=True)


def _attn_kernel(q_ref, k_ref, v_ref, g_ref, lq1_ref, lk1_ref, lq2_ref, lk2_ref, sw_ref, o_ref, *,
                 lambda_init):
    s_len = q_ref.shape[0]
    d = DIFF_HEAD_DIM
    dv = DIFF_V_DIM
    bq, bk = _ATTN_BQ, _ATTN_BK
    lam = (jnp.exp(jnp.sum(lq1_ref[...] * lk1_ref[...], axis=-1, keepdims=True))
           - jnp.exp(jnp.sum(lq2_ref[...] * lk2_ref[...], axis=-1, keepdims=True)) + lambda_init)
    key_idx = lax.broadcasted_iota(jnp.int32, (bk, bq), 0)
    query_idx = lax.broadcasted_iota(jnp.int32, (bk, bq), 1)
    out_scale = sw_ref[...] * (1.0 - lambda_init)
    v_t = [v_ref[:, hd * dv:(hd + 1) * dv].astype(F32).T.astype(BF16)
           for hd in range(_ATTN_HEADS_PER_STEP)]

    def combine(p_blocks, t, weights):
        (e1, f1), (e2, f2) = weights[0][t], weights[1][t]
        return (p_blocks or []) + [(e1 * f1 - e2 * f2).astype(BF16)]

    def finish(hd, lo, hi, p_blocks):
        cols = slice(hd * dv, (hd + 1) * dv)
        o = _dot(v_t[hd][:, 0:hi], jnp.concatenate(p_blocks, axis=0)).T
        ms = jnp.mean(o * o, axis=-1, keepdims=True)
        o = o * lax.rsqrt(ms + EPS) * out_scale
        o_ref[lo:hi, cols] = (o * _silu(g_ref[lo:hi, cols].astype(F32))).astype(o_ref.dtype)

    n_tiles = s_len // bq
    tiles = [(hd, i) for hd in range(_ATTN_HEADS_PER_STEP) for i in list(range(1, n_tiles)) + [0]]
    pending = None
    for hd, i in tiles:
        lo, hi = i * bq, (i + 1) * bq
        qk0 = hd * 2 * d
        q = q_ref[lo:hi, qk0:qk0 + 2 * d]
        n_blocks = hi // bk
        n_comb = 0 if pending is None else len(pending[3][0])
        blocks = ([], [])
        p_blocks = None
        for t in range(n_blocks):
            for j in range(2):
                blk = _dot_nt(k_ref[t * bk:(t + 1) * bk, qk0 + j * d:qk0 + (j + 1) * d],
                              q[:, j * d:(j + 1) * d])
                if (t + 1) * bk - 1 > lo:
                    blk = jnp.where(key_idx + (t * bk - lo) <= query_idx, blk, _MASKED)
                blocks[j].append(_softmax_block(blk))
            for tc in range(t * n_comb // n_blocks, (t + 1) * n_comb // n_blocks):
                p_blocks = combine(p_blocks, tc, pending[3])
        if pending is not None:
            finish(pending[0], pending[1], pending[2], p_blocks)
        weights = []
        for j in range(2):
            m_all = functools.reduce(jnp.maximum, [b[1] for b in blocks[j]])
            alphas = [jnp.exp2(b[1] - m_all) for b in blocks[j]]
            total = sum(a * b[2] for a, b in zip(alphas, blocks[j]))
            norm = (1.0 / total) if j == 0 else (lam / total)
            weights.append([(b[0], a * norm) for a, b in zip(alphas, blocks[j])])
        pending = (hd, lo, hi, weights)
    p_blocks = None
    for t in range(len(pending[3][0])):
        p_blocks = combine(p_blocks, t, pending[3])
    finish(pending[0], pending[1], pending[2], p_blocks)


def _diff_attention(proj, lq1, lk1, lq2, lk2, subln_w, lambda_init):
    bsz, s, _ = proj.shape
    width = _ATTN_HEADS_PER_STEP * DIFF_V_DIM
    steps = DIFF_HEADS // _ATTN_HEADS_PER_STEP
    blk = 5 * _nbytes((s, width), BF16)
    vec = lambda a: a.reshape(1, -1)
    const = lambda b, h: (0, 0)
    section = lambda n: (lambda b, h: (b, 0, n * steps + h))
    return pl.pallas_call(
        functools.partial(_attn_kernel, lambda_init=lambda_init),
        grid=(bsz, steps),
        in_specs=[pl.BlockSpec((None, s, width), section(0)),
                  pl.BlockSpec((None, s, width), section(1)),
                  pl.BlockSpec((None, s, width), section(2)),
                  pl.BlockSpec((None, s, width), section(3)),
                  pl.BlockSpec((1, DIFF_HEAD_DIM), const),
                  pl.BlockSpec((1, DIFF_HEAD_DIM), const),
                  pl.BlockSpec((1, DIFF_HEAD_DIM), const),
                  pl.BlockSpec((1, DIFF_HEAD_DIM), const),
                  pl.BlockSpec((1, DIFF_V_DIM), const)],
        out_specs=pl.BlockSpec((None, s, width), section(0)),
        out_shape=jax.ShapeDtypeStruct((bsz, s, DIFF_WIDTH), BF16),
        compiler_params=pltpu.CompilerParams(
            dimension_semantics=("arbitrary", "arbitrary"),
            vmem_limit_bytes=_vmem_limit(blk, temp_bytes=8 * _nbytes((_ATTN_BQ, s), F32))),
        name="diff_attn",
    )(proj, proj, proj, proj, vec(lq1), vec(lk1), vec(lq2), vec(lk2), vec(subln_w))


def kernel(x, norm_w, even_w_in, even_conv_w, even_conv_b, even_dt_bias, even_a_log, even_d_skip,
           even_ssd_norm_w, even_sgu_ln_w, even_sgu_ln_b, even_sgu_ws, even_sgu_b, even_w_out,
           odd_w_in, odd_lam_q1, odd_lam_k1, odd_lam_q2, odd_lam_k2, odd_subln_w, odd_w_out,
           final_norm_w):
    bsz, s, d = x.shape
    m = bsz * s
    x2 = x.reshape(m, d)

    w_in_t = jnp.swapaxes(even_w_in, 1, 2)
    hn, dt_raw = _rmsnorm_dt(x2, norm_w[0], w_in_t)
    n_main = even_w_in.shape[2] - SSD_HEADS
    proj = _in_proj(hn, w_in_t, n_main, transposed=True, skip_col=SSD_DT_COL,
                    skip=SSD_HEADS)
    y_a = _ssd(proj.reshape(bsz, s, -1), dt_raw.reshape(bsz, s, DT_PAD), even_conv_w[0], even_conv_b[0],
               even_dt_bias[0], even_a_log[0], even_d_skip[0], even_ssd_norm_w[0])
    h1, hn1 = _sgu_out_proj(y_a.reshape(m, SSD_WIDTH), proj, 3, even_sgu_ln_w[0], even_sgu_ln_b[0],
                            even_sgu_ws[0], even_sgu_b[0], even_w_out[0].astype(BF16), x2, norm_w[1])

    lambda_init = 0.8 - 0.6 * math.exp(-0.3 * 1)
    n_in1 = odd_w_in.shape[2]
    q_scale = jnp.where(jnp.arange(n_in1) < DIFF_WIDTH, ATTN_Q_SCALE, 1.0).astype(F32)
    proj1 = _in_proj(hn1, odd_w_in, n_in1, col_scale=q_scale)
    o = _diff_attention(proj1.reshape(bsz, s, -1), odd_lam_q1[0], odd_lam_k1[0], odd_lam_q2[0],
                        odd_lam_k2[0], odd_subln_w[0], lambda_init)
    out = _out_proj_norm(o.reshape(m, DIFF_WIDTH), odd_w_out[0].astype(BF16), h1, final_norm_w)
    return out.reshape(bsz, s, d)
```

```python
import functools
import math

import jax
import jax.numpy as jnp
from jax import lax
from jax.experimental import pallas as pl
from jax.experimental.pallas import tpu as pltpu

F32 = jnp.float32
BF16 = jnp.bfloat16

D_MODEL = 2048
SSD_WIDTH = 2048
SSD_HEAD_DIM = 64
SSD_HEADS = 32
SSD_GROUPS = 8
SSD_HEADS_PER_GROUP = SSD_HEADS // SSD_GROUPS
SSD_GROUP_WIDTH = SSD_WIDTH // SSD_GROUPS
SSD_STATE = 128
SSD_CONV = 4
SSD_CHUNK = 128
SSD_BC_WIDTH = SSD_GROUPS * SSD_STATE
SSD_DT_COL = 2 * SSD_WIDTH + 2 * SSD_BC_WIDTH
SGU_WIDTH = 2048
SGU_CHUNK = 128
SGU_GROUPS = 16
SGU_GROUP_DIM = 128
DIFF_HEADS = 16
DIFF_HEAD_DIM = 128
DIFF_V_DIM = 256
DIFF_WIDTH = DIFF_HEADS * DIFF_V_DIM
EPS = 1e-6

V7X_LANES = 128
V7X_VMEM_BYTES = 64 * 1024 * 1024
DT_PAD = V7X_LANES


def _vmem_limit(block_bytes, scratch_bytes=0, temp_bytes=0):
    est = 2 * block_bytes + scratch_bytes + temp_bytes + (4 << 20)
    return int(min(est, V7X_VMEM_BYTES - (8 << 20)))


def _nbytes(shape, dtype):
    return math.prod(shape) * jnp.dtype(dtype).itemsize


def _silu(x):
    h = 0.5 * x
    return h + h * jnp.tanh(h)


def _gelu_tanh(x):
    c = math.sqrt(2.0 / math.pi)
    h = 0.5 * x
    return h + h * jnp.tanh(x * (c + (c * 0.044715) * (x * x)))


def _split_bf16(x, n):
    parts = []
    r = x
    for i in range(n):
        p = r.astype(BF16)
        parts.append(p)
        if i + 1 < n:
            r = r - p.astype(F32)
    return parts


def _dot(a, b):
    return jnp.dot(a, b, preferred_element_type=F32)


def _dot_nt(a, b):
    return lax.dot_general(a, b, (((1,), (1,)), ((), ())), preferred_element_type=F32)


def _rms_dt_kernel(x_ref, w_ref, wdt_ref, o_ref, dt_ref):
    x = x_ref[...]
    ms = jnp.mean(x * x, axis=-1, keepdims=True)
    y = x * lax.rsqrt(ms + EPS) * w_ref[...]
    y_hi, y_lo = _split_bf16(y, 2)
    o_ref[...] = y_hi
    row = lax.broadcasted_iota(jnp.int32, wdt_ref.shape, 0)
    w_hi, w_lo = _split_bf16(jnp.where(row < SSD_HEADS, wdt_ref[...], 0.0), 2)
    dt_ref[...] = _dot_nt(y_hi, w_hi) + (_dot_nt(y_lo, w_hi) + _dot_nt(y_hi, w_lo))


def _rmsnorm_dt(x, w, w_in_t, bm=512):
    m, d = x.shape
    blk = _nbytes((bm, d), F32) + _nbytes((bm, d), BF16) + _nbytes((DT_PAD, d), F32)
    return pl.pallas_call(
        _rms_dt_kernel,
        grid=(m // bm,),
        in_specs=[pl.BlockSpec((bm, d), lambda i: (i, 0)),
                  pl.BlockSpec((1, d), lambda i: (0, 0)),
                  pl.BlockSpec((None, DT_PAD, d), lambda i: (0, SSD_DT_COL // DT_PAD, 0))],
        out_specs=[pl.BlockSpec((bm, d), lambda i: (i, 0)),
                   pl.BlockSpec((bm, DT_PAD), lambda i: (i, 0))],
        out_shape=[jax.ShapeDtypeStruct((m, d), BF16),
                   jax.ShapeDtypeStruct((m, DT_PAD), F32)],
        compiler_params=pltpu.CompilerParams(
            dimension_semantics=("arbitrary",),
            vmem_limit_bytes=_vmem_limit(blk, temp_bytes=4 * _nbytes((bm, d), F32))),
        name="rmsnorm_dt",
    )(x, w.reshape(1, d), w_in_t)


_SIDE_CAST_ROWS = 128


_IN_PROJ_ROW_PIECES = 2


def _in_proj_kernel(x_ref, w_ref, *rest, transposed, scaled, side_cast):
    rest = list(rest)
    scale_ref = rest.pop(0) if scaled else None
    cast_src_ref = rest.pop(0) if side_cast else None
    o_ref = rest.pop(0)
    cast_dst_ref = rest.pop(0) if side_cast else None
    (wbf_ref,) = rest

    first = pl.program_id(1) == 0
    piece = o_ref.shape[0] // _IN_PROJ_ROW_PIECES

    def project(w):
        for r in range(0, o_ref.shape[0], piece):
            acc = (_dot_nt if transposed else _dot)(x_ref[r:r + piece, :], w)
            if scaled:
                acc = acc * scale_ref[...]
            o_ref[r:r + piece, :] = acc.astype(o_ref.dtype)

    @pl.when(first)
    def _():
        w = w_ref[...].astype(BF16)
        wbf_ref[...] = w
        project(w)

    @pl.when(jnp.logical_not(first))
    def _():
        project(wbf_ref[...])

    if side_cast:
        cast_dst_ref[...] = cast_src_ref[...].astype(BF16)


def _in_proj(x, w3, n_out, transposed=False, skip_col=None, skip=0, col_scale=None, side_cast=None,
             bm=2048, bn=1024):
    m, k = x.shape
    if transposed:
        assert skip % 8 == 0 and (skip == 0 or skip_col % bn == 0)
        start = lambda j: 8 * (j * (bn // 8) + (jnp.where(j * bn >= skip_col, skip // 8, 0) if skip else 0))
        w_spec = pl.BlockSpec((None, pl.Element(bn), pl.Element(k)), lambda j, i: (0, start(j), 0))
        w_block = (bn, k)
    else:
        assert not skip
        w_spec = pl.BlockSpec((None, k, bn), lambda j, i: (0, 0, j))
        w_block = (k, bn)
    grid = (n_out // bn, m // bm)
    in_specs = [pl.BlockSpec((bm, k), lambda j, i: (i, 0)), w_spec]
    args = [x, w3]
    out_specs = [pl.BlockSpec((bm, bn), lambda j, i: (i, j))]
    out_shape = [jax.ShapeDtypeStruct((m, n_out), BF16)]
    blk = _nbytes((bm, k), BF16) + _nbytes(w_block, F32) + _nbytes((bm, bn), BF16)
    if col_scale is not None:
        in_specs.append(pl.BlockSpec((1, bn), lambda j, i: (0, j)))
        args.append(col_scale.reshape(1, n_out))
    if side_cast is not None:
        _, rows, cols = side_cast.shape
        cast_rows = _SIDE_CAST_ROWS
        n_cast = rows // cast_rows
        assert rows % cast_rows == 0 and n_cast <= grid[0] * grid[1]
        cast_idx = lambda j, i: (jnp.minimum(j * grid[1] + i, n_cast - 1), 0)
        in_specs.append(pl.BlockSpec((None, cast_rows, cols), lambda j, i: (0,) + cast_idx(j, i)))
        args.append(side_cast)
        out_specs.append(pl.BlockSpec((cast_rows, cols), cast_idx))
        out_shape.append(jax.ShapeDtypeStruct((rows, cols), BF16))
        blk += _nbytes((cast_rows, cols), F32) + _nbytes((cast_rows, cols), BF16)
    out = pl.pallas_call(
        functools.partial(_in_proj_kernel, transposed=transposed, scaled=col_scale is not None,
                          side_cast=side_cast is not None),
        grid=grid,
        in_specs=in_specs,
        out_specs=out_specs,
        out_shape=out_shape,
        scratch_shapes=[pltpu.VMEM(w_block, BF16)],
        compiler_params=pltpu.CompilerParams(
            dimension_semantics=("arbitrary", "arbitrary"),
            vmem_limit_bytes=_vmem_limit(blk, _nbytes(w_block, BF16),
                                         temp_bytes=_nbytes((bm // _IN_PROJ_ROW_PIECES, bn), F32))),
        name="in_proj",
    )(*args)
    return out if side_cast is not None else out[0]


def _out_proj_norm_kernel(xa_ref, xb_ref, wa_ref, wb_ref, r_ref, nw_ref, o_ref):
    h = r_ref[...] + (_dot(xa_ref[...], wa_ref[...]) + _dot(xb_ref[...], wb_ref[...]))
    ms = jnp.mean(h * h, axis=-1, keepdims=True)
    o_ref[...] = (h * lax.rsqrt(ms + EPS) * nw_ref[...]).astype(o_ref.dtype)


def _out_proj_norm(x, w, res, norm_w, bm=512):
    m = x.shape[0]
    k = w.shape[0] // 2
    n = w.shape[1]
    blk = 2 * _nbytes((bm, k), BF16) + 2 * _nbytes((bm, n), F32)
    row = lambda i: (i, 0)
    once = pl.Buffered(1)
    return pl.pallas_call(
        _out_proj_norm_kernel,
        grid=(m // bm,),
        in_specs=[pl.BlockSpec((bm, k), row),
                  pl.BlockSpec((bm, k), lambda i: (i, 1)),
                  pl.BlockSpec((k, n), lambda i: (0, 0), pipeline_mode=once),
                  pl.BlockSpec((k, n), lambda i: (1, 0), pipeline_mode=once),
                  pl.BlockSpec((bm, n), row),
                  pl.BlockSpec((1, n), lambda i: (0, 0))],
        out_specs=pl.BlockSpec((bm, n), row),
        out_shape=jax.ShapeDtypeStruct((m, n), F32),
        compiler_params=pltpu.CompilerParams(
            dimension_semantics=("arbitrary",),
            vmem_limit_bytes=_vmem_limit(blk, 2 * _nbytes((k, n), BF16), temp_bytes=3 * _nbytes((bm, n), F32))),
        name="out_proj_norm",
    )(x, x, w, w, res, norm_w.reshape(1, n))


_CONV_TAIL = 16
_CONV_COLS = 512


def _conv_shift_matrix():
    L = SSD_CHUNK
    r = jnp.arange((SSD_CONV - 1) * L)
    src = _CONV_TAIL + (r % L) - (SSD_CONV - 1 - r // L)
    return (src[:, None] == jnp.arange(_CONV_TAIL + L)[None, :]).astype(BF16)


_SSD_CHUNKS_PER_STEP = 4


def _ssd_kernel(z_ref, xs_ref, bc_ref, dtr_ref, cw_ref, cb_ref, dtb_ref, alog_ref, dskip_ref,
                nw_ref, exp_ref, shift_ref, o_ref, tail_ref, xc_ref, state_ref):
    @pl.when(pl.program_id(1) == 0)
    def _():
        state_ref[...] = jnp.zeros_like(state_ref)
        tail_ref[...] = jnp.zeros_like(tail_ref)

    for sub in range(_SSD_CHUNKS_PER_STEP):
        rows = pl.ds(sub * SSD_CHUNK, SSD_CHUNK)
        _ssd_chunk(z_ref.at[rows], xs_ref.at[rows], bc_ref.at[rows], dtr_ref.at[rows], cw_ref, cb_ref, dtb_ref,
                   alog_ref, dskip_ref, nw_ref, exp_ref, shift_ref, o_ref.at[rows], tail_ref, xc_ref.at[sub],
                   state_ref)


def _ssd_chunk(z_ref, xs_ref, bc_ref, dtr_ref, cw_ref, cb_ref, dtb_ref, alog_ref, dskip_ref,
               nw_ref, exp_ref, shift_ref, o_ref, tail_ref, xc_ref, state_ref):
    L = SSD_CHUNK

    def conv_chunk(j):
        cols = slice(j * _CONV_COLS, (j + 1) * _CONV_COLS)
        src_ref, c_src = (xs_ref, j * _CONV_COLS) if j * _CONV_COLS < SSD_WIDTH else (bc_ref, j * _CONV_COLS - SSD_WIDTH)
        cur = src_ref[:, c_src:c_src + _CONV_COLS]
        shifted = _dot(shift_ref[...], jnp.concatenate([tail_ref[:, cols], cur], axis=0))
        acc = cb_ref[:, cols] + cw_ref[SSD_CONV - 1:SSD_CONV, cols] * cur.astype(F32)
        for k in range(SSD_CONV - 1):
            acc = acc + cw_ref[k:k + 1, cols] * shifted[k * L:(k + 1) * L, :]
        xc_ref[:, cols] = _silu(acc)
        tail_ref[:, cols] = cur[L - _CONV_TAIL:L, :]

    def conv_chunks_of(g):
        return (g * SSD_GROUP_WIDTH // _CONV_COLS,
                (SSD_WIDTH + g * SSD_STATE) // _CONV_COLS,
                (SSD_WIDTH + SSD_BC_WIDTH + g * SSD_STATE) // _CONV_COLS)

    conv_emitted = set()

    def conv_for(g):
        for j in conv_chunks_of(g):
            if j not in conv_emitted:
                conv_emitted.add(j)
                conv_chunk(j)

    conv_for(0)

    x_dt = dtr_ref[...] + dtb_ref[...]
    dt = jnp.maximum(x_dt, 0.0) + jnp.log1p(jnp.exp(-jnp.abs(x_dt)))
    da = dt * (-jnp.exp(alog_ref[...]))
    row = lax.broadcasted_iota(jnp.int32, (L, L), 0)
    col = lax.broadcasted_iota(jnp.int32, (L, L), 1)
    causal = col <= row
    tri = jnp.where(causal, 1.0, 0.0).astype(BF16)
    cs = sum(_dot(tri, p) for p in _split_bf16(da, 3))
    cs_t = cs.T
    cs_last = cs[L - 1:L, :]
    exp_cs = jnp.exp(cs)
    w_state = jnp.exp(cs_last - cs) * dt
    chunk_decay = jnp.broadcast_to(jnp.exp(cs_last), (8, DT_PAD))

    per_head = _split_bf16(jnp.concatenate([dt, w_state, chunk_decay], axis=0), 2)

    def expand(gcols):
        return sum(_dot(p, exp_ref[:, gcols]) for p in per_head)

    lane = lax.broadcasted_iota(jnp.int32, (2 * L, SSD_GROUP_WIDTH), 1)

    def operands(g):
        conv_for(g)
        gcols = slice(g * SSD_GROUP_WIDTH, (g + 1) * SSD_GROUP_WIDTH)
        xg = xc_ref[:, gcols]
        b0 = SSD_WIDTH + g * SSD_STATE
        c0 = SSD_WIDTH + SSD_BC_WIDTH + g * SSD_STATE
        bg = xc_ref[:, b0:b0 + SSD_STATE].astype(BF16)
        cg = xc_ref[:, c0:c0 + SSD_STATE]
        cbm = jnp.where(causal, _dot_nt(cg.astype(BF16), bg), 0.0)
        prev = state_ref[g]
        per_chan = expand(gcols)
        rhs_all = jnp.concatenate([(xg * per_chan[0:L]).astype(BF16), prev.astype(BF16)], axis=0)
        lhs_parts, rhs_parts = [], []
        for r in range(SSD_HEADS_PER_GROUP):
            h = g * SSD_HEADS_PER_GROUP + r
            seg = jnp.where(causal, cs[:, h:h + 1] - cs_t[h:h + 1, :], 0.0)
            lhs_parts.append((cbm * jnp.exp(seg)).astype(BF16))
            lhs_parts.append((cg * exp_cs[:, h:h + 1]).astype(BF16))
            in_head = (lane >= r * SSD_HEAD_DIM) & (lane < (r + 1) * SSD_HEAD_DIM)
            rhs_parts.append(jnp.where(in_head, rhs_all, jnp.zeros_like(rhs_all)))
        xs_w = (xg * per_chan[L:2 * L]).astype(BF16)
        return (xg, bg, xs_w, prev * per_chan[2 * L:2 * L + 1],
                jnp.concatenate(lhs_parts, axis=1), jnp.concatenate(rhs_parts, axis=0))

    def outputs(g, xg, bg, xs_w, decayed_state, lhs, rhs):
        gcols = slice(g * SSD_GROUP_WIDTH, (g + 1) * SSD_GROUP_WIDTH)
        y = _dot(lhs, rhs)
        y = y + xg * dskip_ref[:, gcols]
        y = y * _silu(z_ref[:, gcols].astype(F32))
        ms = jnp.mean(y * y, axis=-1, keepdims=True)
        o_ref[:, gcols] = (y * lax.rsqrt(ms + EPS) * nw_ref[:, gcols]).astype(o_ref.dtype)
        state_ref[g] = decayed_state + _dot(bg.T, xs_w)

    ops = operands(0)
    for g in range(SSD_GROUPS):
        nxt = operands(g + 1) if g + 1 < SSD_GROUPS else None
        outputs(g, *ops)
        ops = nxt


def _ssd(proj, dt_raw, conv_w, conv_b, dt_bias, a_log, d_skip, norm_w):
    bsz, s, _ = proj.shape
    L = _SSD_CHUNKS_PER_STEP * SSD_CHUNK
    conv_dim = SSD_WIDTH + 2 * SSD_BC_WIDTH
    pad = DT_PAD - SSD_HEADS
    head_of_col = jnp.arange(SSD_WIDTH) // SSD_HEAD_DIM
    expand_mat = (jnp.arange(DT_PAD)[:, None] == head_of_col[None, :]).astype(BF16)
    blk = (3 * _nbytes((L, SSD_WIDTH), BF16) + _nbytes((L, DT_PAD), F32) + _nbytes((L, SSD_WIDTH), BF16)
           + _nbytes((SSD_CONV + 1, conv_dim), F32) + _nbytes((DT_PAD, SSD_WIDTH), BF16))
    shift_mat = _conv_shift_matrix()
    blk += _nbytes(shift_mat.shape, BF16)
    scratch = (_nbytes((_CONV_TAIL, conv_dim), BF16) + _nbytes((L, conv_dim), F32)
               + _nbytes((SSD_GROUPS, SSD_STATE, SSD_GROUP_WIDTH), F32))
    row = lambda b, c: (b, c, 0)
    const = lambda b, c: (0, 0)
    return pl.pallas_call(
        _ssd_kernel,
        grid=(bsz, s // L),
        in_specs=[pl.BlockSpec((None, L, SSD_WIDTH), lambda b, c: (b, c, 0)),
                  pl.BlockSpec((None, L, SSD_WIDTH), lambda b, c: (b, c, 1)),
                  pl.BlockSpec((None, L, SSD_WIDTH), lambda b, c: (b, c, 2)),
                  pl.BlockSpec((None, L, DT_PAD), row),
                  pl.BlockSpec((SSD_CONV, conv_dim), const),
                  pl.BlockSpec((1, conv_dim), const),
                  pl.BlockSpec((1, DT_PAD), const),
                  pl.BlockSpec((1, DT_PAD), const),
                  pl.BlockSpec((1, SSD_WIDTH), const),
                  pl.BlockSpec((1, SSD_WIDTH), const),
                  pl.BlockSpec((DT_PAD, SSD_WIDTH), const),
                  pl.BlockSpec(shift_mat.shape, const)],
        out_specs=pl.BlockSpec((None, L, SSD_WIDTH), row),
        out_shape=jax.ShapeDtypeStruct((bsz, s, SSD_WIDTH), BF16),
        scratch_shapes=[pltpu.VMEM((_CONV_TAIL, conv_dim), BF16),
                        pltpu.VMEM((_SSD_CHUNKS_PER_STEP, SSD_CHUNK, conv_dim), F32),
                        pltpu.VMEM((SSD_GROUPS, SSD_STATE, SSD_GROUP_WIDTH), F32)],
        compiler_params=pltpu.CompilerParams(
            dimension_semantics=("arbitrary", "arbitrary"),
            vmem_limit_bytes=_vmem_limit(blk, scratch, temp_bytes=16 << 20)),
        name="ssd",
    )(proj, proj, proj, dt_raw, conv_w, conv_b.reshape(1, conv_dim),
      jnp.pad(dt_bias, (0, pad)).reshape(1, DT_PAD), jnp.pad(a_log, (0, pad)).reshape(1, DT_PAD),
      jnp.repeat(d_skip, SSD_HEAD_DIM).reshape(1, SSD_WIDTH), norm_w.reshape(1, SSD_WIDTH), expand_mat,
      shift_mat)


_SGU_CHUNKS_PER_STEP = 2
_SGU_OUT_PROJ_PIECES = 8


def _sgu_rows(zb_ref, u_ref, v_ref, lnw_ref, lnb_ref, ws_ref, bias_ref, o_ref, after_group):
    T = SGU_CHUNK
    v = _gelu_tanh(v_ref[...].astype(F32))
    mu = jnp.mean(v, axis=-1, keepdims=True)
    vc = v - mu
    var = jnp.mean(vc * vc, axis=-1, keepdims=True)
    vn = (vc * lax.rsqrt(var + EPS) * lnw_ref[...] + lnb_ref[...]).astype(BF16)
    row = lax.broadcasted_iota(jnp.int32, (T, T), 0)
    col = lax.broadcasted_iota(jnp.int32, (T, T), 1)
    mixed_all = []
    for g in range(SGU_GROUPS):
        gcols = slice(g * SGU_GROUP_DIM, (g + 1) * SGU_GROUP_DIM)
        ws = jnp.where(col <= row, ws_ref[g], 0.0).astype(BF16)
        rhs = jnp.concatenate([vn[k * T:(k + 1) * T, gcols] for k in range(_SGU_CHUNKS_PER_STEP)], axis=1)
        mixed_all.append(_dot(ws, rhs) + bias_ref[:, g:g + 1])
    for g in range(SGU_GROUPS):
        gcols = slice(g * SGU_GROUP_DIM, (g + 1) * SGU_GROUP_DIM)
        mixed = mixed_all[g]
        for k in range(_SGU_CHUNKS_PER_STEP):
            rows = slice(k * T, (k + 1) * T)
            u = _gelu_tanh(u_ref[rows, gcols].astype(F32))
            gate = _silu(zb_ref[rows, gcols].astype(F32))
            o_ref[rows, gcols] = (u * mixed[:, k * SGU_GROUP_DIM:(k + 1) * SGU_GROUP_DIM] * gate).astype(o_ref.dtype)
        after_group(g)


def _sgu_out_proj_kernel(ya_ref, zb_ref, u_ref, v_ref, lnw_ref, lnb_ref, ws_ref, bias_ref, wa_ref, wb_ref,
                         r_ref, nw_ref, h_ref, hn_ref, yb_ref):
    cw = h_ref.shape[1] // _SGU_OUT_PROJ_PIECES

    def after_group(g):
        for c in range(g * _SGU_OUT_PROJ_PIECES // SGU_GROUPS, (g + 1) * _SGU_OUT_PROJ_PIECES // SGU_GROUPS):
            cc = slice(c * cw, (c + 1) * cw)
            h_ref[:, cc] = r_ref[:, cc] + _dot(ya_ref[...], wa_ref[:, cc])

    _sgu_rows(zb_ref, u_ref, v_ref, lnw_ref, lnb_ref, ws_ref, bias_ref, yb_ref, after_group)
    h = h_ref[...] + _dot(yb_ref[...], wb_ref[...])
    h_ref[...] = h
    ms = jnp.mean(h * h, axis=-1, keepdims=True)
    hn_ref[...] = (h * lax.rsqrt(ms + EPS) * nw_ref[...]).astype(hn_ref.dtype)


def _sgu_out_proj(ya, proj2d, col0, ln_w, ln_b, ws, bias, w, res, norm_w):
    m, k = ya.shape
    n = w.shape[1]
    bm = _SGU_CHUNKS_PER_STEP * SGU_CHUNK
    blk = (4 * _nbytes((bm, k), BF16) + _nbytes((SGU_GROUPS, SGU_CHUNK, SGU_CHUNK), F32)
           + 2 * _nbytes((bm, n), F32) + _nbytes((bm, n), BF16))
    row = lambda i: (i, 0)
    const2 = lambda i: (0, 0)
    once = pl.Buffered(1)
    return pl.pallas_call(
        _sgu_out_proj_kernel,
        grid=(m // bm,),
        in_specs=[pl.BlockSpec((bm, k), row),
                  pl.BlockSpec((bm, SGU_WIDTH), lambda i: (i, col0)),
                  pl.BlockSpec((bm, SGU_WIDTH), lambda i: (i, col0 + 1)),
                  pl.BlockSpec((bm, SGU_WIDTH), lambda i: (i, col0 + 2)),
                  pl.BlockSpec((1, SGU_WIDTH), const2),
                  pl.BlockSpec((1, SGU_WIDTH), const2),
                  pl.BlockSpec((SGU_GROUPS, SGU_CHUNK, SGU_CHUNK), lambda i: (0, 0, 0)),
                  pl.BlockSpec((SGU_CHUNK, SGU_GROUPS), const2),
                  pl.BlockSpec((k, n), lambda i: (0, 0), pipeline_mode=once),
                  pl.BlockSpec((SGU_WIDTH, n), lambda i: (k // SGU_WIDTH, 0), pipeline_mode=once),
                  pl.BlockSpec((bm, n), row),
                  pl.BlockSpec((1, n), const2)],
        out_specs=[pl.BlockSpec((bm, n), row), pl.BlockSpec((bm, n), row)],
        out_shape=[jax.ShapeDtypeStruct((m, n), F32), jax.ShapeDtypeStruct((m, n), BF16)],
        scratch_shapes=[pltpu.VMEM((bm, SGU_WIDTH), BF16)],
        compiler_params=pltpu.CompilerParams(
            dimension_semantics=("arbitrary",),
            vmem_limit_bytes=_vmem_limit(blk, _nbytes((k + SGU_WIDTH, n), BF16) + _nbytes((bm, SGU_WIDTH), BF16),
                                         temp_bytes=8 * _nbytes((bm, SGU_WIDTH), F32))),
        name="sgu_out_proj",
    )(ya, proj2d, proj2d, proj2d, ln_w.reshape(1, SGU_WIDTH), ln_b.reshape(1, SGU_WIDTH), ws, bias.T,
      w, w, res, norm_w.reshape(1, n))


_ATTN_BQ = 256
_ATTN_BK = 256
_ATTN_HEADS_PER_STEP = 2
_MASKED = -1e30


ATTN_Q_SCALE = DIFF_HEAD_DIM ** -0.5 * math.log2(math.e)


def _softmax_block(st):
    m = jnp.max(st, axis=0, keepdims=True)
    e = jnp.exp2(st - m)
    return e, m, jnp.sum(e, axis=0, keepdims=True)


def _attn_kernel(q_ref, k_ref, v_ref, g_ref, lq1_ref, lk1_ref, lq2_ref, lk2_ref, sw_ref, o_ref, *,
                 lambda_init):
    s_len = q_ref.shape[0]
    d = DIFF_HEAD_DIM
    dv = DIFF_V_DIM
    bq, bk = _ATTN_BQ, _ATTN_BK
    lam = (jnp.exp(jnp.sum(lq1_ref[...] * lk1_ref[...], axis=-1, keepdims=True))
           - jnp.exp(jnp.sum(lq2_ref[...] * lk2_ref[...], axis=-1, keepdims=True)) + lambda_init)
    key_idx = lax.broadcasted_iota(jnp.int32, (bk, bq), 0)
    query_idx = lax.broadcasted_iota(jnp.int32, (bk, bq), 1)
    out_scale = sw_ref[...] * (1.0 - lambda_init)
    v_t = [v_ref[:, hd * dv:(hd + 1) * dv].astype(F32).T.astype(BF16)
           for hd in range(_ATTN_HEADS_PER_STEP)]

    def combine(p_blocks, t, weights):
        (e1, f1), (e2, f2) = weights[0][t], weights[1][t]
        return (p_blocks or []) + [(e1 * f1 - e2 * f2).astype(BF16)]

    def finish(hd, lo, hi, p_blocks):
        cols = slice(hd * dv, (hd + 1) * dv)
        o = _dot(v_t[hd][:, 0:hi], jnp.concatenate(p_blocks, axis=0)).T
        ms = jnp.mean(o * o, axis=-1, keepdims=True)
        o = o * lax.rsqrt(ms + EPS) * out_scale
        o_ref[lo:hi, cols] = (o * _silu(g_ref[lo:hi, cols].astype(F32))).astype(o_ref.dtype)

    n_tiles = s_len // bq
    tiles = [(hd, i) for hd in range(_ATTN_HEADS_PER_STEP) for i in list(range(1, n_tiles)) + [0]]
    pending = None
    for hd, i in tiles:
        lo, hi = i * bq, (i + 1) * bq
        qk0 = hd * 2 * d
        q = q_ref[lo:hi, qk0:qk0 + 2 * d]
        n_blocks = hi // bk
        n_comb = 0 if pending is None else len(pending[3][0])
        blocks = ([], [])
        p_blocks = None
        for t in range(n_blocks):
            for j in range(2):
                blk = _dot_nt(k_ref[t * bk:(t + 1) * bk, qk0 + j * d:qk0 + (j + 1) * d],
                              q[:, j * d:(j + 1) * d])
                if (t + 1) * bk - 1 > lo:
                    blk = jnp.where(key_idx + (t * bk - lo) <= query_idx, blk, _MASKED)
                blocks[j].append(_softmax_block(blk))
            for tc in range(t * n_comb // n_blocks, (t + 1) * n_comb // n_blocks):
                p_blocks = combine(p_blocks, tc, pending[3])
        if pending is not None:
            finish(pending[0], pending[1], pending[2], p_blocks)
        weights = []
        for j in range(2):
            m_all = functools.reduce(jnp.maximum, [b[1] for b in blocks[j]])
            alphas = [jnp.exp2(b[1] - m_all) for b in blocks[j]]
            total = sum(a * b[2] for a, b in zip(alphas, blocks[j]))
            norm = (1.0 / total) if j == 0 else (lam / total)
            weights.append([(b[0], a * norm) for a, b in zip(alphas, blocks[j])])
        pending = (hd, lo, hi, weights)
    p_blocks = None
    for t in range(len(pending[3][0])):
        p_blocks = combine(p_blocks, t, pending[3])
    finish(pending[0], pending[1], pending[2], p_blocks)


def _diff_attention(proj, lq1, lk1, lq2, lk2, subln_w, lambda_init):
    bsz, s, _ = proj.shape
    width = _ATTN_HEADS_PER_STEP * DIFF_V_DIM
    steps = DIFF_HEADS // _ATTN_HEADS_PER_STEP
    blk = 5 * _nbytes((s, width), BF16)
    vec = lambda a: a.reshape(1, -1)
    const = lambda b, h: (0, 0)
    section = lambda n: (lambda b, h: (b, 0, n * steps + h))
    return pl.pallas_call(
        functools.partial(_attn_kernel, lambda_init=lambda_init),
        grid=(bsz, steps),
        in_specs=[pl.BlockSpec((None, s, width), section(0)),
                  pl.BlockSpec((None, s, width), section(1)),
                  pl.BlockSpec((None, s, width), section(2)),
                  pl.BlockSpec((None, s, width), section(3)),
                  pl.BlockSpec((1, DIFF_HEAD_DIM), const),
                  pl.BlockSpec((1, DIFF_HEAD_DIM), const),
                  pl.BlockSpec((1, DIFF_HEAD_DIM), const),
                  pl.BlockSpec((1, DIFF_HEAD_DIM), const),
                  pl.BlockSpec((1, DIFF_V_DIM), const)],
        out_specs=pl.BlockSpec((None, s, width), section(0)),
        out_shape=jax.ShapeDtypeStruct((bsz, s, DIFF_WIDTH), BF16),
        compiler_params=pltpu.CompilerParams(
            dimension_semantics=("arbitrary", "arbitrary"),
            vmem_limit_bytes=_vmem_limit(blk, temp_bytes=8 * _nbytes((_ATTN_BQ, s), F32))),
        name="diff_attn",
    )(proj, proj, proj, proj, vec(lq1), vec(lk1), vec(lq2), vec(lk2), vec(subln_w))


def kernel(x, norm_w, even_w_in, even_conv_w, even_conv_b, even_dt_bias, even_a_log, even_d_skip,
           even_ssd_norm_w, even_sgu_ln_w, even_sgu_ln_b, even_sgu_ws, even_sgu_b, even_w_out,
           odd_w_in, odd_lam_q1, odd_lam_k1, odd_lam_q2, odd_lam_k2, odd_subln_w, odd_w_out,
           final_norm_w):
    bsz, s, d = x.shape
    m = bsz * s
    x2 = x.reshape(m, d)

    w_in_t = jnp.swapaxes(even_w_in, 1, 2)
    hn, dt_raw = _rmsnorm_dt(x2, norm_w[0], w_in_t)
    n_main = even_w_in.shape[2] - SSD_HEADS
    proj, w_out0 = _in_proj(hn, w_in_t, n_main, transposed=True, skip_col=SSD_DT_COL, skip=SSD_HEADS,
                            side_cast=even_w_out)
    y_a = _ssd(proj.reshape(bsz, s, -1), dt_raw.reshape(bsz, s, DT_PAD), even_conv_w[0], even_conv_b[0],
               even_dt_bias[0], even_a_log[0], even_d_skip[0], even_ssd_norm_w[0])
    h1, hn1 = _sgu_out_proj(y_a.reshape(m, SSD_WIDTH), proj, 3, even_sgu_ln_w[0], even_sgu_ln_b[0],
                            even_sgu_ws[0], even_sgu_b[0], w_out0, x2, norm_w[1])

    lambda_init = 0.8 - 0.6 * math.exp(-0.3 * 1)
    n_in1 = odd_w_in.shape[2]
    q_scale = jnp.where(jnp.arange(n_in1) < DIFF_WIDTH, ATTN_Q_SCALE, 1.0).astype(F32)
    proj1, w_out1 = _in_proj(hn1, odd_w_in, n_in1, col_scale=q_scale, side_cast=odd_w_out)
    o = _diff_attention(proj1.reshape(bsz, s, -1), odd_lam_q1[0], odd_lam_k1[0], odd_lam_q2[0],
                        odd_lam_k2[0], odd_subln_w[0], lambda_init)
    out = _out_proj_norm(o.reshape(m, DIFF_WIDTH), w_out1, h1, final_norm_w)
    return out.reshape(bsz, s, d)
```

```python
import functools
import math

import jax
import jax.numpy as jnp
from jax import lax
from jax.experimental import pallas as pl
from jax.experimental.pallas import tpu as pltpu

F32 = jnp.float32
BF16 = jnp.bfloat16

D_MODEL = 2048
SSD_WIDTH = 2048
SSD_HEAD_DIM = 64
SSD_HEADS = 32
SSD_GROUPS = 8
SSD_HEADS_PER_GROUP = SSD_HEADS // SSD_GROUPS
SSD_GROUP_WIDTH = SSD_WIDTH // SSD_GROUPS
SSD_STATE = 128
SSD_CONV = 4
SSD_CHUNK = 128
SSD_BC_WIDTH = SSD_GROUPS * SSD_STATE
SSD_DT_COL = 2 * SSD_WIDTH + 2 * SSD_BC_WIDTH
SGU_WIDTH = 2048
SGU_CHUNK = 128
SGU_GROUPS = 16
SGU_GROUP_DIM = 128
DIFF_HEADS = 16
DIFF_HEAD_DIM = 128
DIFF_V_DIM = 256
DIFF_WIDTH = DIFF_HEADS * DIFF_V_DIM
EPS = 1e-6

V7X_LANES = 128
V7X_VMEM_BYTES = 64 * 1024 * 1024
DT_PAD = V7X_LANES


def _vmem_limit(block_bytes, scratch_bytes=0, temp_bytes=0):
    est = 2 * block_bytes + scratch_bytes + temp_bytes + (4 << 20)
    return int(min(est, V7X_VMEM_BYTES - (8 << 20)))


def _nbytes(shape, dtype):
    return math.prod(shape) * jnp.dtype(dtype).itemsize


def _silu(x):
    h = 0.5 * x
    return h + h * jnp.tanh(h)


def _gelu_tanh(x):
    c = math.sqrt(2.0 / math.pi)
    h = 0.5 * x
    return h + h * jnp.tanh(x * (c + (c * 0.044715) * (x * x)))


def _split_bf16(x, n):
    parts = []
    r = x
    for i in range(n):
        p = r.astype(BF16)
        parts.append(p)
        if i + 1 < n:
            r = r - p.astype(F32)
    return parts


def _dot(a, b):
    return jnp.dot(a, b, preferred_element_type=F32)


def _dot_nt(a, b):
    return lax.dot_general(a, b, (((1,), (1,)), ((), ())), preferred_element_type=F32)


def _rms_dt_kernel(x_ref, w_ref, wdt_ref, o_ref, dt_ref):
    x = x_ref[...]
    ms = jnp.mean(x * x, axis=-1, keepdims=True)
    y = x * lax.rsqrt(ms + EPS) * w_ref[...]
    y_hi, y_lo = _split_bf16(y, 2)
    o_ref[...] = y_hi
    row = lax.broadcasted_iota(jnp.int32, wdt_ref.shape, 0)
    w_hi, w_lo = _split_bf16(jnp.where(row < SSD_HEADS, wdt_ref[...], 0.0), 2)
    dt_ref[...] = _dot_nt(y_hi, w_hi) + (_dot_nt(y_lo, w_hi) + _dot_nt(y_hi, w_lo))


def _rmsnorm_dt(x, w, w_in_t, bm=512):
    m, d = x.shape
    blk = _nbytes((bm, d), F32) + _nbytes((bm, d), BF16) + _nbytes((DT_PAD, d), F32)
    return pl.pallas_call(
        _rms_dt_kernel,
        grid=(m // bm,),
        in_specs=[pl.BlockSpec((bm, d), lambda i: (i, 0)),
                  pl.BlockSpec((1, d), lambda i: (0, 0)),
                  pl.BlockSpec((None, DT_PAD, d), lambda i: (0, SSD_DT_COL // DT_PAD, 0))],
        out_specs=[pl.BlockSpec((bm, d), lambda i: (i, 0)),
                   pl.BlockSpec((bm, DT_PAD), lambda i: (i, 0))],
        out_shape=[jax.ShapeDtypeStruct((m, d), BF16),
                   jax.ShapeDtypeStruct((m, DT_PAD), F32)],
        compiler_params=pltpu.CompilerParams(
            dimension_semantics=("arbitrary",),
            vmem_limit_bytes=_vmem_limit(blk, temp_bytes=4 * _nbytes((bm, d), F32))),
        name="rmsnorm_dt",
    )(x, w.reshape(1, d), w_in_t)


_SIDE_CAST_ROWS = 128


_IN_PROJ_ROW_PIECES = 2


def _in_proj_kernel(x_ref, w_ref, *rest, transposed, scaled, side_cast):
    rest = list(rest)
    scale_ref = rest.pop(0) if scaled else None
    cast_src_ref = rest.pop(0) if side_cast else None
    o_ref = rest.pop(0)
    cast_dst_ref = rest.pop(0) if side_cast else None
    (wbf_ref,) = rest

    first = pl.program_id(1) == 0
    piece = o_ref.shape[0] // _IN_PROJ_ROW_PIECES

    def project(w):
        for r in range(0, o_ref.shape[0], piece):
            acc = (_dot_nt if transposed else _dot)(x_ref[r:r + piece, :], w)
            if scaled:
                acc = acc * scale_ref[...]
            o_ref[r:r + piece, :] = acc.astype(o_ref.dtype)

    @pl.when(first)
    def _():
        w = w_ref[...].astype(BF16)
        wbf_ref[...] = w
        project(w)

    @pl.when(jnp.logical_not(first))
    def _():
        project(wbf_ref[...])

    if side_cast:
        cast_dst_ref[...] = cast_src_ref[...].astype(BF16)


def _in_proj(x, w3, n_out, transposed=False, skip_col=None, skip=0, col_scale=None, side_cast=None,
             bm=2048, bn=1024):
    m, k = x.shape
    if transposed:
        assert skip % 8 == 0 and (skip == 0 or skip_col % bn == 0)
        start = lambda j: 8 * (j * (bn // 8) + (jnp.where(j * bn >= skip_col, skip // 8, 0) if skip else 0))
        w_spec = pl.BlockSpec((None, pl.Element(bn), pl.Element(k)), lambda j, i: (0, start(j), 0))
        w_block = (bn, k)
    else:
        assert not skip
        w_spec = pl.BlockSpec((None, k, bn), lambda j, i: (0, 0, j))
        w_block = (k, bn)
    grid = (n_out // bn, m // bm)
    in_specs = [pl.BlockSpec((bm, k), lambda j, i: (i, 0)), w_spec]
    args = [x, w3]
    out_specs = [pl.BlockSpec((bm, bn), lambda j, i: (i, j))]
    out_shape = [jax.ShapeDtypeStruct((m, n_out), BF16)]
    blk = _nbytes((bm, k), BF16) + _nbytes(w_block, F32) + _nbytes((bm, bn), BF16)
    if col_scale is not None:
        in_specs.append(pl.BlockSpec((1, bn), lambda j, i: (0, j)))
        args.append(col_scale.reshape(1, n_out))
    if side_cast is not None:
        _, rows, cols = side_cast.shape
        cast_rows = _SIDE_CAST_ROWS
        n_cast = rows // cast_rows
        assert rows % cast_rows == 0 and n_cast <= grid[0] * grid[1]
        cast_idx = lambda j, i: (jnp.minimum(j * grid[1] + i, n_cast - 1), 0)
        in_specs.append(pl.BlockSpec((None, cast_rows, cols), lambda j, i: (0,) + cast_idx(j, i)))
        args.append(side_cast)
        out_specs.append(pl.BlockSpec((cast_rows, cols), cast_idx))
        out_shape.append(jax.ShapeDtypeStruct((rows, cols), BF16))
        blk += _nbytes((cast_rows, cols), F32) + _nbytes((cast_rows, cols), BF16)
    out = pl.pallas_call(
        functools.partial(_in_proj_kernel, transposed=transposed, scaled=col_scale is not None,
                          side_cast=side_cast is not None),
        grid=grid,
        in_specs=in_specs,
        out_specs=out_specs,
        out_shape=out_shape,
        scratch_shapes=[pltpu.VMEM(w_block, BF16)],
        compiler_params=pltpu.CompilerParams(
            dimension_semantics=("arbitrary", "arbitrary"),
            vmem_limit_bytes=_vmem_limit(blk, _nbytes(w_block, BF16),
                                         temp_bytes=_nbytes((bm // _IN_PROJ_ROW_PIECES, bn), F32))),
        name="in_proj",
    )(*args)
    return out if side_cast is not None else out[0]


def _out_proj_norm_kernel(xa_ref, xb_ref, wa_ref, wb_ref, r_ref, nw_ref, o_ref):
    h = r_ref[...] + (_dot(xa_ref[...], wa_ref[...]) + _dot(xb_ref[...], wb_ref[...]))
    ms = jnp.mean(h * h, axis=-1, keepdims=True)
    o_ref[...] = (h * lax.rsqrt(ms + EPS) * nw_ref[...]).astype(o_ref.dtype)


def _out_proj_norm(x, w, res, norm_w, bm=512):
    m = x.shape[0]
    k = w.shape[0] // 2
    n = w.shape[1]
    blk = 2 * _nbytes((bm, k), BF16) + 2 * _nbytes((bm, n), F32)
    row = lambda i: (i, 0)
    once = pl.Buffered(1)
    return pl.pallas_call(
        _out_proj_norm_kernel,
        grid=(m // bm,),
        in_specs=[pl.BlockSpec((bm, k), row),
                  pl.BlockSpec((bm, k), lambda i: (i, 1)),
                  pl.BlockSpec((k, n), lambda i: (0, 0), pipeline_mode=once),
                  pl.BlockSpec((k, n), lambda i: (1, 0), pipeline_mode=once),
                  pl.BlockSpec((bm, n), row),
                  pl.BlockSpec((1, n), lambda i: (0, 0))],
        out_specs=pl.BlockSpec((bm, n), row),
        out_shape=jax.ShapeDtypeStruct((m, n), F32),
        compiler_params=pltpu.CompilerParams(
            dimension_semantics=("arbitrary",),
            vmem_limit_bytes=_vmem_limit(blk, 2 * _nbytes((k, n), BF16), temp_bytes=3 * _nbytes((bm, n), F32))),
        name="out_proj_norm",
    )(x, x, w, w, res, norm_w.reshape(1, n))


_CONV_TAIL = 16
_CONV_COLS = 512


def _conv_shift_matrix():
    L = SSD_CHUNK
    r = jnp.arange((SSD_CONV - 1) * L)
    src = _CONV_TAIL + (r % L) - (SSD_CONV - 1 - r // L)
    return (src[:, None] == jnp.arange(_CONV_TAIL + L)[None, :]).astype(BF16)


_SSD_CHUNKS_PER_STEP = 4


def _ssd_kernel(z_ref, xs_ref, bc_ref, dtr_ref, cw_ref, cb_ref, dtb_ref, alog_ref, dskip_ref,
                nw_ref, exp_ref, shift_ref, o_ref, tail_ref, xc_ref, state_ref):
    @pl.when(pl.program_id(1) == 0)
    def _():
        state_ref[...] = jnp.zeros_like(state_ref)
        tail_ref[...] = jnp.zeros_like(tail_ref)

    for sub in range(_SSD_CHUNKS_PER_STEP):
        rows = pl.ds(sub * SSD_CHUNK, SSD_CHUNK)
        _ssd_chunk(z_ref.at[rows], xs_ref.at[rows], bc_ref.at[rows], dtr_ref.at[rows], cw_ref, cb_ref, dtb_ref,
                   alog_ref, dskip_ref, nw_ref, exp_ref, shift_ref, o_ref.at[rows], tail_ref, xc_ref.at[sub],
                   state_ref)


def _ssd_chunk(z_ref, xs_ref, bc_ref, dtr_ref, cw_ref, cb_ref, dtb_ref, alog_ref, dskip_ref,
               nw_ref, exp_ref, shift_ref, o_ref, tail_ref, xc_ref, state_ref):
    L = SSD_CHUNK

    def conv_chunk(j):
        cols = slice(j * _CONV_COLS, (j + 1) * _CONV_COLS)
        src_ref, c_src = (xs_ref, j * _CONV_COLS) if j * _CONV_COLS < SSD_WIDTH else (bc_ref, j * _CONV_COLS - SSD_WIDTH)
        cur = src_ref[:, c_src:c_src + _CONV_COLS]
        shifted = _dot(shift_ref[...], jnp.concatenate([tail_ref[:, cols], cur], axis=0))
        acc = cb_ref[:, cols] + cw_ref[SSD_CONV - 1:SSD_CONV, cols] * cur.astype(F32)
        for k in range(SSD_CONV - 1):
            acc = acc + cw_ref[k:k + 1, cols] * shifted[k * L:(k + 1) * L, :]
        xc_ref[:, cols] = _silu(acc)
        tail_ref[:, cols] = cur[L - _CONV_TAIL:L, :]

    def conv_chunks_of(g):
        return (g * SSD_GROUP_WIDTH // _CONV_COLS,
                (SSD_WIDTH + g * SSD_STATE) // _CONV_COLS,
                (SSD_WIDTH + SSD_BC_WIDTH + g * SSD_STATE) // _CONV_COLS)

    conv_emitted = set()

    def conv_for(g):
        for j in conv_chunks_of(g):
            if j not in conv_emitted:
                conv_emitted.add(j)
                conv_chunk(j)

    conv_for(0)

    x_dt = dtr_ref[...] + dtb_ref[...]
    dt = jnp.maximum(x_dt, 0.0) + jnp.log1p(jnp.exp(-jnp.abs(x_dt)))
    da = dt * (-jnp.exp(alog_ref[...]))
    row = lax.broadcasted_iota(jnp.int32, (L, L), 0)
    col = lax.broadcasted_iota(jnp.int32, (L, L), 1)
    causal = col <= row
    tri = jnp.where(causal, 1.0, 0.0).astype(BF16)
    cs = sum(_dot(tri, p) for p in _split_bf16(da, 3))
    cs_t = cs.T
    cs_last = cs[L - 1:L, :]
    exp_cs = jnp.exp(cs)
    w_state = jnp.exp(cs_last - cs) * dt
    chunk_decay = jnp.broadcast_to(jnp.exp(cs_last), (8, DT_PAD))

    per_head = _split_bf16(jnp.concatenate([dt, w_state, chunk_decay], axis=0), 2)

    def expand(gcols):
        return sum(_dot(p, exp_ref[:, gcols]) for p in per_head)

    lane = lax.broadcasted_iota(jnp.int32, (2 * L, SSD_GROUP_WIDTH), 1)

    def operands(g):
        conv_for(g)
        gcols = slice(g * SSD_GROUP_WIDTH, (g + 1) * SSD_GROUP_WIDTH)
        xg = xc_ref[:, gcols]
        b0 = SSD_WIDTH + g * SSD_STATE
        c0 = SSD_WIDTH + SSD_BC_WIDTH + g * SSD_STATE
        bg = xc_ref[:, b0:b0 + SSD_STATE].astype(BF16)
        cg = xc_ref[:, c0:c0 + SSD_STATE]
        cbm = jnp.where(causal, _dot_nt(cg.astype(BF16), bg), 0.0)
        prev = state_ref[g]
        per_chan = expand(gcols)
        rhs_all = jnp.concatenate([(xg * per_chan[0:L]).astype(BF16), prev.astype(BF16)], axis=0)
        lhs_parts, rhs_parts = [], []
        for r in range(SSD_HEADS_PER_GROUP):
            h = g * SSD_HEADS_PER_GROUP + r
            seg = jnp.where(causal, cs[:, h:h + 1] - cs_t[h:h + 1, :], 0.0)
            lhs_parts.append((cbm * jnp.exp(seg)).astype(BF16))
            lhs_parts.append((cg * exp_cs[:, h:h + 1]).astype(BF16))
            in_head = (lane >= r * SSD_HEAD_DIM) & (lane < (r + 1) * SSD_HEAD_DIM)
            rhs_parts.append(jnp.where(in_head, rhs_all, jnp.zeros_like(rhs_all)))
        xs_w = (xg * per_chan[L:2 * L]).astype(BF16)
        return (xg, bg, xs_w, prev * per_chan[2 * L:2 * L + 1],
                jnp.concatenate(lhs_parts, axis=1), jnp.concatenate(rhs_parts, axis=0))

    def outputs(g, xg, bg, xs_w, decayed_state, lhs, rhs):
        gcols = slice(g * SSD_GROUP_WIDTH, (g + 1) * SSD_GROUP_WIDTH)
        y = _dot(lhs, rhs)
        y = y + xg * dskip_ref[:, gcols]
        y = y * _silu(z_ref[:, gcols].astype(F32))
        ms = jnp.mean(y * y, axis=-1, keepdims=True)
        o_ref[:, gcols] = (y * lax.rsqrt(ms + EPS) * nw_ref[:, gcols]).astype(o_ref.dtype)
        state_ref[g] = decayed_state + _dot(bg.T, xs_w)

    ops = operands(0)
    for g in range(SSD_GROUPS):
        nxt = operands(g + 1) if g + 1 < SSD_GROUPS else None
        outputs(g, *ops)
        ops = nxt


def _ssd(proj, dt_raw, conv_w, conv_b, dt_bias, a_log, d_skip, norm_w):
    bsz, s, _ = proj.shape
    L = _SSD_CHUNKS_PER_STEP * SSD_CHUNK
    conv_dim = SSD_WIDTH + 2 * SSD_BC_WIDTH
    pad = DT_PAD - SSD_HEADS
    head_of_col = jnp.arange(SSD_WIDTH) // SSD_HEAD_DIM
    expand_mat = (jnp.arange(DT_PAD)[:, None] == head_of_col[None, :]).astype(BF16)
    blk = (3 * _nbytes((L, SSD_WIDTH), BF16) + _nbytes((L, DT_PAD), F32) + _nbytes((L, SSD_WIDTH), BF16)
           + _nbytes((SSD_CONV + 1, conv_dim), F32) + _nbytes((DT_PAD, SSD_WIDTH), BF16))
    shift_mat = _conv_shift_matrix()
    blk += _nbytes(shift_mat.shape, BF16)
    scratch = (_nbytes((_CONV_TAIL, conv_dim), BF16) + _nbytes((L, conv_dim), F32)
               + _nbytes((SSD_GROUPS, SSD_STATE, SSD_GROUP_WIDTH), F32))
    row = lambda b, c: (b, c, 0)
    const = lambda b, c: (0, 0)
    return pl.pallas_call(
        _ssd_kernel,
        grid=(bsz, s // L),
        in_specs=[pl.BlockSpec((None, L, SSD_WIDTH), lambda b, c: (b, c, 0)),
                  pl.BlockSpec((None, L, SSD_WIDTH), lambda b, c: (b, c, 1)),
                  pl.BlockSpec((None, L, SSD_WIDTH), lambda b, c: (b, c, 2)),
                  pl.BlockSpec((None, L, DT_PAD), row),
                  pl.BlockSpec((SSD_CONV, conv_dim), const),
                  pl.BlockSpec((1, conv_dim), const),
                  pl.BlockSpec((1, DT_PAD), const),
                  pl.BlockSpec((1, DT_PAD), const),
                  pl.BlockSpec((1, SSD_WIDTH), const),
                  pl.BlockSpec((1, SSD_WIDTH), const),
                  pl.BlockSpec((DT_PAD, SSD_WIDTH), const),
                  pl.BlockSpec(shift_mat.shape, const)],
        out_specs=pl.BlockSpec((None, L, SSD_WIDTH), row),
        out_shape=jax.ShapeDtypeStruct((bsz, s, SSD_WIDTH), BF16),
        scratch_shapes=[pltpu.VMEM((_CONV_TAIL, conv_dim), BF16),
                        pltpu.VMEM((_SSD_CHUNKS_PER_STEP, SSD_CHUNK, conv_dim), F32),
                        pltpu.VMEM((SSD_GROUPS, SSD_STATE, SSD_GROUP_WIDTH), F32)],
        compiler_params=pltpu.CompilerParams(
            dimension_semantics=("arbitrary", "arbitrary"),
            vmem_limit_bytes=_vmem_limit(blk, scratch, temp_bytes=16 << 20)),
        name="ssd",
    )(proj, proj, proj, dt_raw, conv_w, conv_b.reshape(1, conv_dim),
      jnp.pad(dt_bias, (0, pad)).reshape(1, DT_PAD), jnp.pad(a_log, (0, pad)).reshape(1, DT_PAD),
      jnp.repeat(d_skip, SSD_HEAD_DIM).reshape(1, SSD_WIDTH), norm_w.reshape(1, SSD_WIDTH), expand_mat,
      shift_mat)


_SGU_CHUNKS_PER_STEP = 2
_SGU_OUT_PROJ_PIECES = 8


def _sgu_rows(zb_ref, u_ref, v_ref, lnw_ref, lnb_ref, ws_ref, bias_ref, o_ref, after_group):
    T = SGU_CHUNK
    v = _gelu_tanh(v_ref[...].astype(F32))
    mu = jnp.mean(v, axis=-1, keepdims=True)
    vc = v - mu
    var = jnp.mean(vc * vc, axis=-1, keepdims=True)
    vn = (vc * lax.rsqrt(var + EPS) * lnw_ref[...] + lnb_ref[...]).astype(BF16)
    row = lax.broadcasted_iota(jnp.int32, (T, T), 0)
    col = lax.broadcasted_iota(jnp.int32, (T, T), 1)
    mixed_all = []
    for g in range(SGU_GROUPS):
        gcols = slice(g * SGU_GROUP_DIM, (g + 1) * SGU_GROUP_DIM)
        ws = jnp.where(col <= row, ws_ref[g], 0.0).astype(BF16)
        rhs = jnp.concatenate([vn[k * T:(k + 1) * T, gcols] for k in range(_SGU_CHUNKS_PER_STEP)], axis=1)
        mixed_all.append(_dot(ws, rhs) + bias_ref[:, g:g + 1])
    for g in range(SGU_GROUPS):
        gcols = slice(g * SGU_GROUP_DIM, (g + 1) * SGU_GROUP_DIM)
        mixed = mixed_all[g]
        for k in range(_SGU_CHUNKS_PER_STEP):
            rows = slice(k * T, (k + 1) * T)
            u = _gelu_tanh(u_ref[rows, gcols].astype(F32))
            gate = _silu(zb_ref[rows, gcols].astype(F32))
            o_ref[rows, gcols] = (u * mixed[:, k * SGU_GROUP_DIM:(k + 1) * SGU_GROUP_DIM] * gate).astype(o_ref.dtype)
        after_group(g)


def _sgu_out_proj_kernel(ya_ref, zb_ref, u_ref, v_ref, lnw_ref, lnb_ref, ws_ref, bias_ref, wa_ref, wb_ref,
                         r_ref, nw_ref, h_ref, hn_ref, yb_ref):
    cw = h_ref.shape[1] // _SGU_OUT_PROJ_PIECES

    def after_group(g):
        for c in range(g * _SGU_OUT_PROJ_PIECES // SGU_GROUPS, (g + 1) * _SGU_OUT_PROJ_PIECES // SGU_GROUPS):
            cc = slice(c * cw, (c + 1) * cw)
            h_ref[:, cc] = r_ref[:, cc] + _dot(ya_ref[...], wa_ref[:, cc])

    _sgu_rows(zb_ref, u_ref, v_ref, lnw_ref, lnb_ref, ws_ref, bias_ref, yb_ref, after_group)
    h = h_ref[...] + _dot(yb_ref[...], wb_ref[...])
    h_ref[...] = h
    ms = jnp.mean(h * h, axis=-1, keepdims=True)
    hn_ref[...] = (h * lax.rsqrt(ms + EPS) * nw_ref[...]).astype(hn_ref.dtype)


def _sgu_out_proj(ya, proj2d, col0, ln_w, ln_b, ws, bias, w, res, norm_w):
    m, k = ya.shape
    n = w.shape[1]
    bm = _SGU_CHUNKS_PER_STEP * SGU_CHUNK
    blk = (4 * _nbytes((bm, k), BF16) + _nbytes((SGU_GROUPS, SGU_CHUNK, SGU_CHUNK), F32)
           + 2 * _nbytes((bm, n), F32) + _nbytes((bm, n), BF16))
    row = lambda i: (i, 0)
    const2 = lambda i: (0, 0)
    once = pl.Buffered(1)
    return pl.pallas_call(
        _sgu_out_proj_kernel,
        grid=(m // bm,),
        in_specs=[pl.BlockSpec((bm, k), row),
                  pl.BlockSpec((bm, SGU_WIDTH), lambda i: (i, col0)),
                  pl.BlockSpec((bm, SGU_WIDTH), lambda i: (i, col0 + 1)),
                  pl.BlockSpec((bm, SGU_WIDTH), lambda i: (i, col0 + 2)),
                  pl.BlockSpec((1, SGU_WIDTH), const2),
                  pl.BlockSpec((1, SGU_WIDTH), const2),
                  pl.BlockSpec((SGU_GROUPS, SGU_CHUNK, SGU_CHUNK), lambda i: (0, 0, 0)),
                  pl.BlockSpec((SGU_CHUNK, SGU_GROUPS), const2),
                  pl.BlockSpec((k, n), lambda i: (0, 0), pipeline_mode=once),
                  pl.BlockSpec((SGU_WIDTH, n), lambda i: (k // SGU_WIDTH, 0), pipeline_mode=once),
                  pl.BlockSpec((bm, n), row),
                  pl.BlockSpec((1, n), const2)],
        out_specs=[pl.BlockSpec((bm, n), row), pl.BlockSpec((bm, n), row)],
        out_shape=[jax.ShapeDtypeStruct((m, n), F32), jax.ShapeDtypeStruct((m, n), BF16)],
        scratch_shapes=[pltpu.VMEM((bm, SGU_WIDTH), BF16)],
        compiler_params=pltpu.CompilerParams(
            dimension_semantics=("arbitrary",),
            vmem_limit_bytes=_vmem_limit(blk, _nbytes((k + SGU_WIDTH, n), BF16) + _nbytes((bm, SGU_WIDTH), BF16),
                                         temp_bytes=8 * _nbytes((bm, SGU_WIDTH), F32))),
        name="sgu_out_proj",
    )(ya, proj2d, proj2d, proj2d, ln_w.reshape(1, SGU_WIDTH), ln_b.reshape(1, SGU_WIDTH), ws, bias.T,
      w, w, res, norm_w.reshape(1, n))


_ATTN_BQ = 256
_ATTN_BK = 256
_ATTN_HEADS_PER_STEP = 2
_MASKED = -0.5 * float(jnp.finfo(jnp.float32).max)


ATTN_Q_SCALE = DIFF_HEAD_DIM ** -0.5 * math.log2(math.e)


def _softmax_block(st):
    m = jnp.max(st, axis=0, keepdims=True)
    e = jnp.exp2(st - m)
    return e.astype(BF16), m, jnp.sum(e, axis=0, keepdims=True)


def _attn_kernel(q_ref, k_ref, v_ref, g_ref, lq1_ref, lk1_ref, lq2_ref, lk2_ref, sw_ref, o_ref, *,
                 lambda_init):
    s_len = q_ref.shape[0]
    d = DIFF_HEAD_DIM
    dv = DIFF_V_DIM
    bq, bk = _ATTN_BQ, _ATTN_BK
    lam = (jnp.exp(jnp.sum(lq1_ref[...] * lk1_ref[...], axis=-1, keepdims=True))
           - jnp.exp(jnp.sum(lq2_ref[...] * lk2_ref[...], axis=-1, keepdims=True)) + lambda_init)
    key_idx = lax.broadcasted_iota(jnp.int32, (bk, bq), 0)
    query_idx = lax.broadcasted_iota(jnp.int32, (bk, bq), 1)
    out_scale = sw_ref[...] * (1.0 - lambda_init)
    v_t = [v_ref[:, hd * dv:(hd + 1) * dv].astype(F32).T.astype(BF16)
           for hd in range(_ATTN_HEADS_PER_STEP)]

    def combine(p_blocks, t, weights):
        (e1, f1), (e2, f2) = weights[0][t], weights[1][t]
        return (p_blocks or []) + [e1 * f1 - e2 * f2]

    def finish(hd, lo, hi, p_blocks):
        cols = slice(hd * dv, (hd + 1) * dv)
        o = _dot(v_t[hd][:, 0:hi], jnp.concatenate(p_blocks, axis=0)).T
        ms = jnp.mean(o * o, axis=-1, keepdims=True)
        o = o * lax.rsqrt(ms + EPS) * out_scale
        o_ref[lo:hi, cols] = (o * _silu(g_ref[lo:hi, cols].astype(F32))).astype(o_ref.dtype)

    heads = range(_ATTN_HEADS_PER_STEP)
    n_tiles = s_len // bq
    pending = None
    for i in list(range(1, n_tiles)) + [0]:
        lo, hi = i * bq, (i + 1) * bq
        n_blocks = hi // bk
        n_comb = 0 if pending is None else len(pending[2][0][0])
        blocks = [([], []) for _ in heads]
        p_blocks = [None for _ in heads]
        for t in range(n_blocks):
            for hd in heads:
                qk0 = hd * 2 * d
                for j in range(2):
                    cols = slice(qk0 + j * d, qk0 + (j + 1) * d)
                    blk = _dot_nt(k_ref[t * bk:(t + 1) * bk, cols], q_ref[lo:hi, cols])
                    if (t + 1) * bk - 1 > lo:
                        blk = jnp.where(key_idx + (t * bk - lo) <= query_idx, blk, _MASKED)
                    blocks[hd][j].append(_softmax_block(blk))
            for tc in range(t * n_comb // n_blocks, (t + 1) * n_comb // n_blocks):
                for hd in heads:
                    p_blocks[hd] = combine(p_blocks[hd], tc, pending[2][hd])
        if pending is not None:
            for hd in heads:
                finish(hd, pending[0], pending[1], p_blocks[hd])
        weights = []
        for hd in heads:
            per_map = []
            for j in range(2):
                m_all = functools.reduce(jnp.maximum, [b[1] for b in blocks[hd][j]])
                alphas = [jnp.exp2(b[1] - m_all) for b in blocks[hd][j]]
                total = sum(a * b[2] for a, b in zip(alphas, blocks[hd][j]))
                norm = (1.0 / total) if j == 0 else (lam / total)
                per_map.append([(b[0], (a * norm).astype(BF16)) for a, b in zip(alphas, blocks[hd][j])])
            weights.append(per_map)
        pending = (lo, hi, weights)
    for hd in heads:
        p_blocks = None
        for t in range(len(pending[2][hd][0])):
            p_blocks = combine(p_blocks, t, pending[2][hd])
        finish(hd, pending[0], pending[1], p_blocks)


def _diff_attention(proj, lq1, lk1, lq2, lk2, subln_w, lambda_init):
    bsz, s, _ = proj.shape
    width = _ATTN_HEADS_PER_STEP * DIFF_V_DIM
    steps = DIFF_HEADS // _ATTN_HEADS_PER_STEP
    blk = 5 * _nbytes((s, width), BF16)
    vec = lambda a: a.reshape(1, -1)
    const = lambda b, h: (0, 0)
    section = lambda n: (lambda b, h: (b, 0, n * steps + h))
    return pl.pallas_call(
        functools.partial(_attn_kernel, lambda_init=lambda_init),
        grid=(bsz, steps),
        in_specs=[pl.BlockSpec((None, s, width), section(0)),
                  pl.BlockSpec((None, s, width), section(1)),
                  pl.BlockSpec((None, s, width), section(2)),
                  pl.BlockSpec((None, s, width), section(3)),
                  pl.BlockSpec((1, DIFF_HEAD_DIM), const),
                  pl.BlockSpec((1, DIFF_HEAD_DIM), const),
                  pl.BlockSpec((1, DIFF_HEAD_DIM), const),
                  pl.BlockSpec((1, DIFF_HEAD_DIM), const),
                  pl.BlockSpec((1, DIFF_V_DIM), const)],
        out_specs=pl.BlockSpec((None, s, width), section(0)),
        out_shape=jax.ShapeDtypeStruct((bsz, s, DIFF_WIDTH), BF16),
        compiler_params=pltpu.CompilerParams(
            dimension_semantics=("arbitrary", "arbitrary"),
            vmem_limit_bytes=_vmem_limit(blk, temp_bytes=8 * _nbytes((_ATTN_BQ, s), F32))),
        name="diff_attn",
    )(proj, proj, proj, proj, vec(lq1), vec(lk1), vec(lq2), vec(lk2), vec(subln_w))


def kernel(x, norm_w, even_w_in, even_conv_w, even_conv_b, even_dt_bias, even_a_log, even_d_skip,
           even_ssd_norm_w, even_sgu_ln_w, even_sgu_ln_b, even_sgu_ws, even_sgu_b, even_w_out,
           odd_w_in, odd_lam_q1, odd_lam_k1, odd_lam_q2, odd_lam_k2, odd_subln_w, odd_w_out,
           final_norm_w):
    bsz, s, d = x.shape
    m = bsz * s
    x2 = x.reshape(m, d)

    w_in_t = jnp.swapaxes(even_w_in, 1, 2)
    hn, dt_raw = _rmsnorm_dt(x2, norm_w[0], w_in_t)
    n_main = even_w_in.shape[2] - SSD_HEADS
    proj, w_out0 = _in_proj(hn, w_in_t, n_main, transposed=True, skip_col=SSD_DT_COL, skip=SSD_HEADS,
                            side_cast=even_w_out)
    y_a = _ssd(proj.reshape(bsz, s, -1), dt_raw.reshape(bsz, s, DT_PAD), even_conv_w[0], even_conv_b[0],
               even_dt_bias[0], even_a_log[0], even_d_skip[0], even_ssd_norm_w[0])
    h1, hn1 = _sgu_out_proj(y_a.reshape(m, SSD_WIDTH), proj, 3, even_sgu_ln_w[0], even_sgu_ln_b[0],
                            even_sgu_ws[0], even_sgu_b[0], w_out0, x2, norm_w[1])

    lambda_init = 0.8 - 0.6 * math.exp(-0.3 * 1)
    n_in1 = odd_w_in.shape[2]
    q_scale = jnp.where(jnp.arange(n_in1) < DIFF_WIDTH, ATTN_Q_SCALE, 1.0).astype(F32)
    proj1, w_out1 = _in_proj(hn1, odd_w_in, n_in1, col_scale=q_scale, side_cast=odd_w_out)
    o = _diff_attention(proj1.reshape(bsz, s, -1), odd_lam_q1[0], odd_lam_k1[0], odd_lam_q2[0],
                        odd_lam_k2[0], odd_subln_w[0], lambda_init)
    out = _out_proj_norm(o.reshape(m, DIFF_WIDTH), w_out1, h1, final_norm_w)
    return out.reshape(bsz, s, d)
```
